```python
import jax, jax.numpy as jnp
from jax import lax
import numpy as np

D_MODEL = 1024
BATCH = 32
SEQ = 256
DEPTH = 1
DEC_BATCH = 8
DEC_SEQ = 4096
PAST_LEN = 512

GRID_W = 64
HEAD_DIM = 64
A_HEADS = 8
A_KV_HEADS = 2
B_HEADS = 8
A_WIDTH = A_HEADS * HEAD_DIM
A_KV_WIDTH = A_KV_HEADS * HEAD_DIM
B_WIDTH = B_HEADS * HEAD_DIM
D_FF = 4 * D_MODEL
WIN_H = 8
WIN_W = 16
Q_BLOCK = 128
ROPE_BASE = 10000.0
ROPE_PAIRS = HEAD_DIM // 4
RMS_EPS = 1e-6
N_MOD = 6
IN_SPLITS = (A_WIDTH, A_KV_WIDTH, A_KV_WIDTH, B_WIDTH, B_WIDTH, B_WIDTH, D_MODEL, D_MODEL)
IN_WIDTH = sum(IN_SPLITS)
SPLIT_POINTS = tuple(int(i) for i in np.cumsum(IN_SPLITS)[:-1])
NEG_INF = -1e30

kernel_name = 'hybrid_dit_gqa_natten_prefix_step'


def _rmsnorm(x, g):
    xf = x.astype(jnp.float32)
    inv = lax.rsqrt(jnp.mean(xf * xf, axis=-1, keepdims=True) + RMS_EPS)
    return (xf * inv).astype(x.dtype) * g


def _modulation(cvec, w_mod, b_mod):
    m = jax.nn.silu(cvec) @ w_mod + b_mod
    return jnp.split(m[:, None, :], N_MOD, axis=-1)


def _axial_rope_tables(t):
    pos = jnp.arange(t, dtype=jnp.int32)
    row = (pos // GRID_W).astype(jnp.float32)
    col = (pos % GRID_W).astype(jnp.float32)
    inv = ROPE_BASE ** (-jnp.arange(ROPE_PAIRS, dtype=jnp.float32) / ROPE_PAIRS)
    ang_r = row[:, None] * inv
    ang_c = col[:, None] * inv
    return jnp.cos(ang_r), jnp.sin(ang_r), jnp.cos(ang_c), jnp.sin(ang_c)


def _rot(x, cos, sin):
    cos = cos[None, :, None, :].astype(x.dtype)
    sin = sin[None, :, None, :].astype(x.dtype)
    x1, x2 = x[..., :ROPE_PAIRS], x[..., ROPE_PAIRS:]
    return jnp.concatenate([x1 * cos - x2 * sin, x1 * sin + x2 * cos], axis=-1)


def _axial_rope(x, tables):
    cr, sr, cc, sc = tables
    half = HEAD_DIM // 2
    return jnp.concatenate([_rot(x[..., :half], cr, sr), _rot(x[..., half:], cc, sc)], axis=-1)


def _block_attention(q, k, v):
    b, t, h, dh = q.shape
    kv = k.shape[2]
    rep = h // kv
    nb = t // Q_BLOCK
    qb = q.reshape(b, nb, Q_BLOCK, kv, rep, dh).transpose(1, 0, 2, 3, 4, 5)
    scale = dh ** -0.5

    def one(qblk):
        s = jnp.einsum('bqgrd,bkgd->bgrqk', qblk, k).astype(jnp.float32) * scale
        p = jax.nn.softmax(s, axis=-1).astype(v.dtype)
        return jnp.einsum('bgrqk,bkgd->bqgrd', p, v)

    o = lax.map(one, qb)
    return o.transpose(1, 0, 2, 3, 4, 5).reshape(b, t, h * dh)


def _neighbourhood_attention(q, k, v, ctx_k, ctx_v, rel_bias):
    b, t, h, dh = q.shape
    rows = t // GRID_W
    wh = min(WIN_H, rows)
    ww = WIN_W
    scale = dh ** -0.5
    qg = q.reshape(b, rows, GRID_W, h, dh)
    kg = k.reshape(b, rows, GRID_W, h, dh)
    vg = v.reshape(b, rows, GRID_W, h, dh)
    col = jnp.arange(GRID_W, dtype=jnp.int32)
    cstart = jnp.clip(col - ww // 2, 0, GRID_W - ww)
    col_mask = (col[None, :] >= cstart[:, None]) & (col[None, :] < cstart[:, None] + ww)
    dc_idx = jnp.clip(col[None, :] - col[:, None] + WIN_W - 1, 0, 2 * WIN_W - 2)
    n_loc = wh * GRID_W

    def row_block(r):
        rs = jnp.clip(r - wh // 2, 0, rows - wh)
        kr = lax.dynamic_slice_in_dim(kg, rs, wh, axis=1)
        vr = lax.dynamic_slice_in_dim(vg, rs, wh, axis=1)
        qr = lax.dynamic_index_in_dim(qg, r, axis=1, keepdims=False)
        s_loc = jnp.einsum('bqhd,bnkhd->bhqnk', qr, kr).astype(jnp.float32) * scale
        dr_idx = rs + jnp.arange(wh, dtype=jnp.int32) - r + WIN_H - 1
        bias = rel_bias[:, dr_idx[None, :, None], dc_idx[:, None, :]].astype(jnp.float32)
        s_loc = jnp.where(col_mask[:, None, :], s_loc + bias[None], NEG_INF).reshape(b, h, GRID_W, n_loc)
        s_ctx = jnp.einsum('bqhd,bkhd->bhqk', qr, ctx_k).astype(jnp.float32) * scale
        p = jax.nn.softmax(jnp.concatenate([s_loc, s_ctx], axis=-1), axis=-1).astype(v.dtype)
        o = jnp.einsum('bhqn,bnhd->bqhd', p[..., :n_loc], vr.reshape(b, n_loc, h, dh))
        return o + jnp.einsum('bhqk,bkhd->bqhd', p[..., n_loc:], ctx_v)

    o = lax.map(row_block, jnp.arange(rows, dtype=jnp.int32))
    return o.transpose(1, 0, 2, 3, 4).reshape(b, t, h * dh)


def _mixer_inputs(h, w_in, q_norm_g, k_norm_g):
    b, t, _ = h.shape
    aq, ak, av, bq, bk, bv, ga, gb = jnp.split(h @ w_in, SPLIT_POINTS, axis=-1)
    aq = _rmsnorm(aq.reshape(b, t, A_HEADS, HEAD_DIM), q_norm_g)
    ak = _rmsnorm(ak.reshape(b, t, A_KV_HEADS, HEAD_DIM), k_norm_g)
    av = av.reshape(b, t, A_KV_HEADS, HEAD_DIM)
    bq = bq.reshape(b, t, B_HEADS, HEAD_DIM)
    bk = bk.reshape(b, t, B_HEADS, HEAD_DIM)
    bv = bv.reshape(b, t, B_HEADS, HEAD_DIM)
    return aq, ak, av, bq, bk, bv, ga, gb


def _merge(a_o, b_o, ga, gb, w_br_a, w_br_b, w_out):
    m = jax.nn.sigmoid(ga) * (a_o @ w_br_a) + jax.nn.sigmoid(gb) * (b_o @ w_br_b)
    return m @ w_out


def _mlp(h, w_mlp_in, w_mlp_out):
    return jnp.square(jax.nn.relu(h @ w_mlp_in)) @ w_mlp_out


def _context_layer(x, c_ctx, lp):
    (w_mod, b_mod, n1, n2, w_in, qg, kg, nat_bias, w_br_a, w_br_b, w_out, w1, w2) = lp
    sh1, sc1, g1, sh2, sc2, g2 = _modulation(c_ctx[None, :], w_mod, b_mod)
    h = _rmsnorm(x, n1) * (1 + sc1) + sh1
    aq, ak, av, bq, bk, bv, ga, gb = _mixer_inputs(h, w_in, qg, kg)
    a_o = _block_attention(aq, ak, av)
    b_o = _block_attention(bq, bk, bv)
    x = x + g1 * _merge(a_o, b_o, ga, gb, w_br_a, w_br_b, w_out)
    h2 = _rmsnorm(x, n2) * (1 + sc2) + sh2
    x = x + g2 * _mlp(h2, w1, w2)
    return x, (ak, av, bk, bv)


def _latent_layer(x, c, ctx_ak, ctx_av, ctx_bk, ctx_bv, lp):
    (w_mod, b_mod, n1, n2, w_in, qg, kg, nat_bias, w_br_a, w_br_b, w_out, w1, w2) = lp
    t = x.shape[1]
    sh1, sc1, g1, sh2, sc2, g2 = _modulation(c, w_mod, b_mod)
    h = _rmsnorm(x, n1) * (1 + sc1) + sh1
    aq, ak, av, bq, bk, bv, ga, gb = _mixer_inputs(h, w_in, qg, kg)
    tables = _axial_rope_tables(t)
    aq = _axial_rope(aq, tables)
    ak = _axial_rope(ak, tables)
    a_o = _block_attention(aq, jnp.concatenate([ctx_ak, ak], axis=1), jnp.concatenate([ctx_av, av], axis=1))
    b_o = _neighbourhood_attention(bq, bk, bv, ctx_bk, ctx_bv, nat_bias)
    x = x + g1 * _merge(a_o, b_o, ga, gb, w_br_a, w_br_b, w_out)
    h2 = _rmsnorm(x, n2) * (1 + sc2) + sh2
    return x + g2 * _mlp(h2, w1, w2)


def setup_inputs(seed: int = 0) -> dict:
    key = jax.random.key(seed)
    ks = jax.random.split(key, 24)
    nrm = jax.random.normal
    f32 = jnp.float32
    d = D_MODEL
    return {
        'x_prompt': nrm(ks[0], (BATCH, SEQ, d), f32),
        'x_sample': nrm(ks[1], (DEC_BATCH, DEC_SEQ, d), f32),
        'cache_a_k': nrm(ks[2], (DEC_BATCH, DEPTH, PAST_LEN, A_KV_HEADS, HEAD_DIM), f32),
        'cache_a_v': nrm(ks[3], (DEC_BATCH, DEPTH, PAST_LEN, A_KV_HEADS, HEAD_DIM), f32),
        'cache_b_k': nrm(ks[4], (DEC_BATCH, DEPTH, PAST_LEN, B_HEADS, HEAD_DIM), f32),
        'cache_b_v': nrm(ks[5], (DEC_BATCH, DEPTH, PAST_LEN, B_HEADS, HEAD_DIM), f32),
        'c': nrm(ks[6], (DEC_BATCH, d), f32),
        'c_ctx': nrm(ks[7], (d,), f32),
        'w_mod': nrm(ks[8], (DEPTH, d, N_MOD * d), f32) * (0.5 * d ** -0.5),
        'b_mod': nrm(ks[9], (DEPTH, N_MOD * d), f32) * 0.01,
        'norm1_g': 1.0 + 0.01 * nrm(ks[10], (DEPTH, d), f32),
        'norm2_g': 1.0 + 0.01 * nrm(ks[11], (DEPTH, d), f32),
        'w_in': nrm(ks[12], (DEPTH, d, IN_WIDTH), f32) * d ** -0.5,
        'q_norm_g': 1.0 + 0.01 * nrm(ks[13], (DEPTH, HEAD_DIM), f32),
        'k_norm_g': 1.0 + 0.01 * nrm(ks[14], (DEPTH, HEAD_DIM), f32),
        'nat_bias': nrm(ks[15], (DEPTH, B_HEADS, 2 * WIN_H - 1, 2 * WIN_W - 1), f32) * 0.1,
        'w_br_a': nrm(ks[16], (DEPTH, A_WIDTH, d), f32) * A_WIDTH ** -0.5,
        'w_br_b': nrm(ks[17], (DEPTH, B_WIDTH, d), f32) * B_WIDTH ** -0.5,
        'w_out': nrm(ks[18], (DEPTH, d, d), f32) * d ** -0.5,
        'w_mlp_in': nrm(ks[19], (DEPTH, d, D_FF), f32) * d ** -0.5,
        'w_mlp_out': nrm(ks[20], (DEPTH, D_FF, d), f32) * D_FF ** -0.5,
        'final_norm_g': 1.0 + 0.01 * nrm(ks[21], (d,), f32),
    }


def reference(x_prompt, x_sample, cache_a_k, cache_a_v, cache_b_k, cache_b_v, c, c_ctx,
              w_mod, b_mod, norm1_g, norm2_g, w_in, q_norm_g, k_norm_g, nat_bias,
              w_br_a, w_br_b, w_out, w_mlp_in, w_mlp_out, final_norm_g):
    xp = x_prompt
    xs = x_sample
    ak_l, av_l, bk_l, bv_l = [], [], [], []
    for l in range(DEPTH):
        lp = (w_mod[l], b_mod[l], norm1_g[l], norm2_g[l], w_in[l], q_norm_g[l], k_norm_g[l],
              nat_bias[l], w_br_a[l], w_br_b[l], w_out[l], w_mlp_in[l], w_mlp_out[l])
        xp, (ak, av, bk, bv) = _context_layer(xp, c_ctx, lp)
        ak_l.append(ak)
        av_l.append(av)
        bk_l.append(bk)
        bv_l.append(bv)
        xs = _latent_layer(xs, c, cache_a_k[:, l], cache_a_v[:, l], cache_b_k[:, l], cache_b_v[:, l], lp)
    y_prompt = _rmsnorm(xp, final_norm_g)
    y_sample = _rmsnorm(xs, final_norm_g)
    new_a_k = jnp.stack(ak_l, axis=1)
    new_a_v = jnp.stack(av_l, axis=1)
    new_b_k = jnp.stack(bk_l, axis=1)
    new_b_v = jnp.stack(bv_l, axis=1)
    return (y_prompt, y_sample, new_a_k, new_a_v, new_b_k, new_b_v)
```

```python
import functools

import numpy as np
import jax
import jax.numpy as jnp
from jax import lax
from jax.experimental import pallas as pl
from jax.experimental.pallas import tpu as pltpu

F32 = jnp.float32
BF16 = jnp.bfloat16

D_MODEL = 1024
GRID_W = 64
HEAD_DIM = 64
A_HEADS = 8
A_KV_HEADS = 2
B_HEADS = 8
A_WIDTH = A_HEADS * HEAD_DIM
A_KV_WIDTH = A_KV_HEADS * HEAD_DIM
B_WIDTH = B_HEADS * HEAD_DIM
D_FF = 4 * D_MODEL
WIN_H = 8
WIN_W = 16
ROPE_BASE = 10000.0
ROPE_PAIRS = HEAD_DIM // 4
RMS_EPS = 1e-6
N_MOD = 6
NEG_INF = -1e30

LANES = 128
PAIRS = A_WIDTH // LANES
QKV_WIDTH = A_WIDTH + 2 * A_KV_WIDTH + 3 * B_WIDTH
GATE_WIDTH = 2 * D_MODEL
NAT_Q_ROWS = 4
NAT_K_ROWS = 12
VMEM_LIMIT = 56 * 1024 * 1024


def _const_spec(shape):
    nd = len(shape)
    return pl.BlockSpec(shape, lambda *_: (0,) * nd, pipeline_mode=pl.Buffered(1))


def _rms_mod(x, g, scale, shift):
    inv = lax.rsqrt(jnp.mean(x * x, axis=-1, keepdims=True) + RMS_EPS)
    return (x * inv) * g * (1.0 + scale) + shift


def _mod_kernel(c_ref, w_ref, b_ref, o_ref):
    cv = c_ref[...]
    s = cv * jax.nn.sigmoid(cv)
    o_ref[...] = jnp.dot(s.astype(BF16), w_ref[...].astype(BF16),
                         preferred_element_type=F32) + b_ref[...]


def _modulation(cvec, w_mod, b_mod):
    n, d = cvec.shape
    width = w_mod.shape[1]
    tn = 1024
    return pl.pallas_call(
        _mod_kernel,
        grid=(width // tn,),
        in_specs=[pl.BlockSpec((n, d), lambda j: (0, 0)),
                  pl.BlockSpec((d, tn), lambda j: (0, j)),
                  pl.BlockSpec((1, tn), lambda j: (0, j))],
        out_specs=pl.BlockSpec((n, tn), lambda j: (0, j)),
        out_shape=jax.ShapeDtypeStruct((n, width), F32),
        compiler_params=pltpu.CompilerParams(dimension_semantics=("arbitrary",),
                                             vmem_limit_bytes=VMEM_LIMIT),
        name="modulation",
    )(cvec, w_mod, b_mod.reshape(1, width))


def _nat_variant_geometry(variant):
    if variant == 0:
        return 0, (lambda i, n: n < WIN_H)
    if variant == 1:
        return -(WIN_H // 2), (lambda i, n: 0 <= n - i < WIN_H)
    return -WIN_H, (lambda i, n: n >= NAT_K_ROWS - WIN_H)


def _bias_kernel(r_ref, o_ref):
    h = pl.program_id(0)
    n_dr = 2 * WIN_H - 1
    n_dc = 2 * WIN_W - 1
    qc = lax.broadcasted_iota(jnp.int32, (GRID_W, LANES), 0)
    lane = lax.broadcasted_iota(jnp.int32, (GRID_W, LANES), 1)
    kc = lane & (GRID_W - 1)
    hi_half = lane >= GRID_W
    dc = kc - qc + (WIN_W - 1)
    cstart = jnp.clip(qc - WIN_W // 2, 0, GRID_W - WIN_W)
    col_ok = (kc >= cstart) & (kc < cstart + WIN_W)
    neg = jnp.full((GRID_W, LANES), NEG_INF, F32)
    base = h * (n_dr * n_dc)
    tiles = []
    for a in range(n_dr):
        val = neg
        for b in range(n_dc):
            val = jnp.where(dc == b, r_ref[base + a * n_dc + b], val)
        tiles.append(jnp.where(col_ok, val, neg))
    for variant in range(3):
        off, valid = _nat_variant_geometry(variant)
        for i in range(NAT_Q_ROWS):
            for p in range(NAT_K_ROWS // 2):
                halves = []
                for n in (2 * p, 2 * p + 1):
                    a = n - i + off + WIN_H - 1
                    halves.append(tiles[a] if valid(i, n) else neg)
                o_ref[variant, 0, i * GRID_W:(i + 1) * GRID_W, p * LANES:(p + 1) * LANES] = (
                    jnp.where(hi_half, halves[1], halves[0]))


def _expand_bias(nat_bias):
    heads = nat_bias.shape[0]
    nq = NAT_Q_ROWS * GRID_W
    nk = NAT_K_ROWS * GRID_W
    return pl.pallas_call(
        _bias_kernel,
        grid=(heads,),
        in_specs=[pl.BlockSpec(memory_space=pltpu.SMEM)],
        out_specs=pl.BlockSpec((3, 1, nq, nk), lambda h: (0, h, 0, 0)),
        out_shape=jax.ShapeDtypeStruct((3, heads, nq, nk), F32),
        compiler_params=pltpu.CompilerParams(dimension_semantics=("arbitrary",),
                                             vmem_limit_bytes=VMEM_LIMIT),
        name="bias_expand",
    )(nat_bias.reshape(-1))


def _pre_kernel(*refs, rope, emit_cache):
    x_ref, mod_ref, n1_ref, w_ref, qg_ref, kg_ref, hm_ref = refs[:7]
    pos = 7
    if rope:
        cos_ref, sin_ref = refs[pos:pos + 2]
        pos += 2
    qa_ref, ka_ref, va_ref, qb_ref, kb_ref, vb_ref = refs[pos:pos + 6]
    pos += 6
    if emit_cache:
        cak_ref, cav_ref, cbk_ref, cbv_ref = refs[pos:pos + 4]

    x = x_ref[0]
    mod = mod_ref[0]
    h = _rms_mod(x, n1_ref[...], mod[1:2], mod[0:1])
    acc = jnp.dot(h.astype(BF16), w_ref[...], preferred_element_type=F32)
    tm = x.shape[0]
    lane = lax.broadcasted_iota(jnp.int32, (tm, LANES), 1)
    lo_half = lane < HEAD_DIM
    hm = hm_ref[...]

    def head_norm(y, g):
        ss = y * y
        hi = ss.astype(BF16)
        lo = (ss - hi.astype(F32)).astype(BF16)
        ms = (jnp.dot(hi, hm, preferred_element_type=F32)
              + jnp.dot(lo, hm, preferred_element_type=F32)) * (1.0 / HEAD_DIM)
        return (y * lax.rsqrt(ms + RMS_EPS)) * g

    def rotary(y):
        partner = jnp.where((lane & ROPE_PAIRS) == 0,
                            pltpu.roll(y, LANES - ROPE_PAIRS, 1), pltpu.roll(y, ROPE_PAIRS, 1))
        return y * cos_ref[...] + partner * sin_ref[...]

    def store_dup(ref, y):
        r = pltpu.roll(y, HEAD_DIM, 1)
        ref[0, :, 0:LANES] = jnp.where(lo_half, y, r).astype(ref.dtype)
        ref[0, :, LANES:2 * LANES] = jnp.where(lo_half, r, y).astype(ref.dtype)

    for c in range(PAIRS):
        sl = slice(c * LANES, (c + 1) * LANES)
        y = head_norm(acc[:, sl], qg_ref[:, sl])
        if rope:
            y = rotary(y)
        qa_ref[0, :, sl] = y.astype(qa_ref.dtype)

    k = head_norm(acc[:, A_WIDTH:A_WIDTH + A_KV_WIDTH], kg_ref[...])
    v = acc[:, A_WIDTH + A_KV_WIDTH:A_WIDTH + 2 * A_KV_WIDTH]
    if emit_cache:
        cak_ref[0] = k
        cav_ref[0] = v
    if rope:
        k = rotary(k)
    store_dup(ka_ref, k)
    store_dup(va_ref, v)

    b0 = A_WIDTH + 2 * A_KV_WIDTH
    bq = acc[:, b0:b0 + B_WIDTH]
    bk = acc[:, b0 + B_WIDTH:b0 + 2 * B_WIDTH]
    bv = acc[:, b0 + 2 * B_WIDTH:b0 + 3 * B_WIDTH]
    qb_ref[0] = (bq * (HEAD_DIM ** -0.5)).astype(qb_ref.dtype)
    kb_ref[0] = bk.astype(kb_ref.dtype)
    vb_ref[0] = bv.astype(vb_ref.dtype)
    if emit_cache:
        cbk_ref[0] = bk
        cbv_ref[0] = bv


def _pre_attention(x, mod, mod_row, n1, w_qkv, qg, kg, hm, rope_tables, emit_cache, tm):
    b, t, d = x.shape
    rope = rope_tables is not None
    tok = lambda w: pl.BlockSpec((1, tm, w), lambda i, j: (i, j, 0))
    in_specs = [tok(d),
                pl.BlockSpec((1, N_MOD, d), lambda i, j: (mod_row(i), 0, 0)),
                _const_spec((1, d)), _const_spec(w_qkv.shape), _const_spec(qg.shape),
                _const_spec(kg.shape), _const_spec(hm.shape)]
    args = [x, mod, n1, w_qkv, qg, kg, hm]
    if rope:
        in_specs += [pl.BlockSpec((tm, LANES), lambda i, j: (j, 0))] * 2
        args += list(rope_tables)
    widths = [A_WIDTH, 2 * LANES, 2 * LANES, B_WIDTH, B_WIDTH, B_WIDTH]
    out_specs = [tok(w) for w in widths]
    out_shape = [jax.ShapeDtypeStruct((b, t, w), BF16) for w in widths]
    if emit_cache:
        cw = [A_KV_WIDTH, A_KV_WIDTH, B_WIDTH, B_WIDTH]
        out_specs += [tok(w) for w in cw]
        out_shape += [jax.ShapeDtypeStruct((b, t, w), F32) for w in cw]
    return pl.pallas_call(
        functools.partial(_pre_kernel, rope=rope, emit_cache=emit_cache),
        grid=(b, t // tm),
        in_specs=in_specs, out_specs=out_specs, out_shape=out_shape,
        compiler_params=pltpu.CompilerParams(dimension_semantics=("arbitrary", "arbitrary"),
                                             vmem_limit_bytes=VMEM_LIMIT),
        name="pre_attention",
    )(*args)


def _pair_softmax_pv(q, ks, vs, biases):
    tq = q.shape[0]
    lane = lax.broadcasted_iota(jnp.int32, (tq, LANES), 1)
    lo_half = lane < HEAD_DIM
    zero = jnp.zeros_like(q)
    outs = []
    for hh in range(2):
        qh = jnp.where(lo_half if hh == 0 else lane >= HEAD_DIM, q, zero)
        ss = []
        for k, bias in zip(ks, biases):
            s = lax.dot_general(qh, k, (((1,), (1,)), ((), ())), preferred_element_type=F32)
            if bias is not None:
                s = s + bias[hh]
            ss.append(s)
        m = functools.reduce(jnp.maximum, [jnp.max(s, axis=1, keepdims=True) for s in ss])
        ps = [jnp.exp(s - m) for s in ss]
        l = functools.reduce(jnp.add, [jnp.sum(p, axis=1, keepdims=True) for p in ps])
        o = functools.reduce(jnp.add, [jnp.dot(p.astype(BF16), v, preferred_element_type=F32)
                                       for p, v in zip(ps, vs)])
        outs.append(o / l)
    return jnp.where(lo_half, outs[0], outs[1])


def _attn_full_kernel(q_ref, *refs):
    o_ref = refs[-1]
    kv = refs[:-1]
    ks = [kv[i][0] for i in range(0, len(kv), 2)]
    vs = [kv[i][0] for i in range(1, len(kv), 2)]
    o_ref[0] = _pair_softmax_pv(q_ref[0], ks, vs, [None] * len(ks)).astype(o_ref.dtype)


def _full_attention(q, kv_sets, kv_pairs_per_block, tq):
    b, t, _ = q.shape
    qspec = pl.BlockSpec((1, tq, LANES), lambda i, c, j: (i, j, c))
    in_specs = [qspec]
    args = [q]
    for k, v in kv_sets:
        spec = pl.BlockSpec((1, k.shape[1], LANES), lambda i, c, j: (i, 0, c // kv_pairs_per_block))
        in_specs += [spec, spec]
        args += [k, v]
    return pl.pallas_call(
        _attn_full_kernel,
        grid=(b, PAIRS, t // tq),
        in_specs=in_specs, out_specs=qspec,
        out_shape=jax.ShapeDtypeStruct(q.shape, BF16),
        compiler_params=pltpu.CompilerParams(dimension_semantics=("arbitrary",) * 3,
                                             vmem_limit_bytes=VMEM_LIMIT),
        name="full_attention",
    )(*args)


def _attn_nat_kernel(q_ref, kw_ref, vw_ref, kc_ref, vc_ref, bias_ref, o_ref, *, rows):
    j = pl.program_id(2)
    ws = jnp.clip(j * NAT_Q_ROWS - WIN_H // 2, 0, rows - NAT_K_ROWS)
    start = pl.multiple_of(ws * GRID_W, GRID_W)
    nk = NAT_K_ROWS * GRID_W
    kw = kw_ref[0, pl.ds(start, nk), :]
    vw = vw_ref[0, pl.ds(start, nk), :]
    bias = bias_ref[0]
    o_ref[0] = _pair_softmax_pv(q_ref[0], [kw, kc_ref[0]], [vw, vc_ref[0]],
                                [bias, None]).astype(o_ref.dtype)


def _nat_attention(q, k, v, ctx_k, ctx_v, bias):
    b, t, _ = q.shape
    rows = t // GRID_W
    nblk = rows // NAT_Q_ROWS
    tq = NAT_Q_ROWS * GRID_W
    nk = NAT_K_ROWS * GRID_W
    qspec = pl.BlockSpec((1, tq, LANES), lambda i, c, j: (i, j, c))
    whole = lambda n: pl.BlockSpec((1, n, LANES), lambda i, c, j: (i, 0, c))

    def bias_map(i, c, j):
        variant = jnp.where(j == 0, 0, jnp.where(j == nblk - 1, 2, 1))
        return (variant, c, 0, 0)

    return pl.pallas_call(
        functools.partial(_attn_nat_kernel, rows=rows),
        grid=(b, PAIRS, nblk),
        in_specs=[qspec, whole(t), whole(t), whole(ctx_k.shape[1]), whole(ctx_v.shape[1]),
                  pl.BlockSpec((1, 2, tq, nk), bias_map)],
        out_specs=qspec,
        out_shape=jax.ShapeDtypeStruct(q.shape, BF16),
        compiler_params=pltpu.CompilerParams(dimension_semantics=("arbitrary",) * 3,
                                             vmem_limit_bytes=VMEM_LIMIT),
        name="nat_attention",
    )(q, k, v, ctx_k, ctx_v, bias)


def _post_kernel(x_ref, ao_ref, bo_ref, mod_ref, n1_ref, n2_ref, fg_ref,
                 wg_ref, wa_ref, wb_ref, wo_ref, w1_ref, w2_ref, y_ref):
    x = x_ref[0]
    mod = mod_ref[0]
    h = _rms_mod(x, n1_ref[...], mod[1:2], mod[0:1])
    gates = jnp.dot(h.astype(BF16), wg_ref[...], preferred_element_type=F32)
    ma = jnp.dot(ao_ref[0], wa_ref[...], preferred_element_type=F32)
    mb = jnp.dot(bo_ref[0], wb_ref[...], preferred_element_type=F32)
    m = jax.nn.sigmoid(gates[:, :D_MODEL]) * ma + jax.nn.sigmoid(gates[:, D_MODEL:]) * mb
    x1 = x + mod[2:3] * jnp.dot(m.astype(BF16), wo_ref[...], preferred_element_type=F32)
    h2 = _rms_mod(x1, n2_ref[...], mod[4:5], mod[3:4])
    u = jnp.maximum(jnp.dot(h2.astype(BF16), w1_ref[...], preferred_element_type=F32), 0.0)
    x2 = x1 + mod[5:6] * jnp.dot((u * u).astype(BF16), w2_ref[...], preferred_element_type=F32)
    inv = lax.rsqrt(jnp.mean(x2 * x2, axis=-1, keepdims=True) + RMS_EPS)
    y_ref[0] = (x2 * inv) * fg_ref[...]


def _post_attention(x, a_o, b_o, mod, mod_row, n1, n2, fg, wg, wa, wb, wo, w1, w2, tm):
    b, t, d = x.shape
    tok = lambda w: pl.BlockSpec((1, tm, w), lambda i, j: (i, j, 0))
    weights = [wg, wa, wb, wo, w1, w2]
    return pl.pallas_call(
        _post_kernel,
        grid=(b, t // tm),
        in_specs=[tok(d), tok(A_WIDTH), tok(B_WIDTH),
                  pl.BlockSpec((1, N_MOD, d), lambda i, j: (mod_row(i), 0, 0)),
                  _const_spec((1, d)), _const_spec((1, d)), _const_spec((1, d))]
                 + [_const_spec(w.shape) for w in weights],
        out_specs=tok(d),
        out_shape=jax.ShapeDtypeStruct(x.shape, F32),
        compiler_params=pltpu.CompilerParams(dimension_semantics=("arbitrary", "arbitrary"),
                                             vmem_limit_bytes=VMEM_LIMIT),
        name="post_attention",
    )(x, a_o, b_o, mod, n1, n2, fg, *weights)


def _rope_tables(t):
    pos = jnp.arange(t, dtype=jnp.int32)
    row = (pos // GRID_W).astype(F32)
    col = (pos % GRID_W).astype(F32)
    inv = ROPE_BASE ** (-jnp.arange(ROPE_PAIRS, dtype=F32) / ROPE_PAIRS)
    ar = row[:, None] * inv
    ac = col[:, None] * inv
    cos = jnp.concatenate([jnp.cos(ar), jnp.cos(ar), jnp.cos(ac), jnp.cos(ac)], axis=-1)
    sin = jnp.concatenate([-jnp.sin(ar), jnp.sin(ar), -jnp.sin(ac), jnp.sin(ac)], axis=-1)
    return jnp.tile(cos, (1, LANES // HEAD_DIM)), jnp.tile(sin, (1, LANES // HEAD_DIM))


def _dup_heads(x):
    b, t, kv, dh = x.shape
    return jnp.broadcast_to(x[:, :, :, None, :], (b, t, kv, 2, dh)).reshape(b, t, kv * 2 * dh).astype(BF16)


def kernel(x_prompt, x_sample, cache_a_k, cache_a_v, cache_b_k, cache_b_v, c, c_ctx, w_mod, b_mod,
           norm1_g, norm2_g, w_in, q_norm_g, k_norm_g, nat_bias, w_br_a, w_br_b, w_out, w_mlp_in,
           w_mlp_out, final_norm_g):
    assert w_mod.shape[0] == 1, "single-layer trunk"
    nb, seq, d = x_prompt.shape
    nd, dseq, _ = x_sample.shape
    past = cache_a_k.shape[2]

    ctx_row = nd
    pad = (-(nd + 1)) % 8
    cvec = jnp.concatenate([c, c_ctx[None, :], jnp.zeros((pad, d), F32)], axis=0)
    mod = _modulation(cvec, w_mod[0], b_mod[0]).reshape(cvec.shape[0], N_MOD, d)

    w_in_b = w_in[0].astype(BF16)
    w_qkv = w_in_b[:, :QKV_WIDTH]
    w_gate = w_in_b[:, QKV_WIDTH:]
    wa = w_br_a[0].astype(BF16)
    wb = w_br_b[0].astype(BF16)
    wo = w_out[0].astype(BF16)
    w1 = w_mlp_in[0].astype(BF16)
    w2 = w_mlp_out[0].astype(BF16)
    n1 = norm1_g[0].reshape(1, d)
    n2 = norm2_g[0].reshape(1, d)
    fg = final_norm_g.reshape(1, d)
    qg = (jnp.tile(q_norm_g[0], A_HEADS) * (HEAD_DIM ** -0.5)).reshape(1, A_WIDTH)
    kg = jnp.tile(k_norm_g[0], A_KV_HEADS).reshape(1, A_KV_WIDTH)
    head_id = np.arange(LANES) // HEAD_DIM
    hm = jnp.asarray(head_id[:, None] == head_id[None, :], dtype=BF16)

    bias = _expand_bias(nat_bias[0])

    ctx_mod_row = lambda i: ctx_row
    (qa, ka, va, qb, kb, vb, new_ak, new_av, new_bk, new_bv) = _pre_attention(
        x_prompt, mod, ctx_mod_row, n1, w_qkv, qg, kg, hm, None, True, seq)
    a_o = _full_attention(qa, [(ka, va)], 2, seq)
    b_o = _full_attention(qb, [(kb, vb)], 1, seq)
    y_prompt = _post_attention(x_prompt, a_o, b_o, mod, ctx_mod_row, n1, n2, fg,
                               w_gate, wa, wb, wo, w1, w2, seq)

    lat_mod_row = lambda i: i
    (qa, ka, va, qb, kb, vb) = _pre_attention(
        x_sample, mod, lat_mod_row, n1, w_qkv, qg, kg, hm, _rope_tables(dseq), False, 256)
    ca_k = _dup_heads(cache_a_k[:, 0])
    ca_v = _dup_heads(cache_a_v[:, 0])
    cb_k = cache_b_k[:, 0].reshape(nd, past, B_WIDTH).astype(BF16)
    cb_v = cache_b_v[:, 0].reshape(nd, past, B_WIDTH).astype(BF16)
    a_o = _full_attention(qa, [(ca_k, ca_v), (ka, va)], 2, 256)
    b_o = _nat_attention(qb, kb, vb, cb_k, cb_v, bias)
    y_sample = _post_attention(x_sample, a_o, b_o, mod, lat_mod_row, n1, n2, fg,
                               w_gate, wa, wb, wo, w1, w2, 256)

    return (y_prompt, y_sample,
            new_ak.reshape(nb, 1, seq, A_KV_HEADS, HEAD_DIM),
            new_av.reshape(nb, 1, seq, A_KV_HEADS, HEAD_DIM),
            new_bk.reshape(nb, 1, seq, B_HEADS, HEAD_DIM),
            new_bv.reshape(nb, 1, seq, B_HEADS, HEAD_DIM))
```

```python
import functools

import numpy as np
import jax
import jax.numpy as jnp
from jax import lax
from jax.experimental import pallas as pl
from jax.experimental.pallas import tpu as pltpu

F32 = jnp.float32
BF16 = jnp.bfloat16

D_MODEL = 1024
GRID_W = 64
HEAD_DIM = 64
A_HEADS = 8
A_KV_HEADS = 2
B_HEADS = 8
A_WIDTH = A_HEADS * HEAD_DIM
A_KV_WIDTH = A_KV_HEADS * HEAD_DIM
B_WIDTH = B_HEADS * HEAD_DIM
D_FF = 4 * D_MODEL
WIN_H = 8
WIN_W = 16
ROPE_BASE = 10000.0
ROPE_PAIRS = HEAD_DIM // 4
RMS_EPS = 1e-6
N_MOD = 6
NEG_INF = -1e30

LANES = 128
PAIRS = A_WIDTH // LANES
QKV_WIDTH = A_WIDTH + 2 * A_KV_WIDTH + 3 * B_WIDTH
GATE_WIDTH = 2 * D_MODEL
NAT_Q_ROWS = 4
NAT_K_ROWS = 12
VMEM_LIMIT = 56 * 1024 * 1024
LOG2E = 1.4426950408889634
Q_SCALE = HEAD_DIM ** -0.5 * LOG2E


def _const_spec(shape):
    nd = len(shape)
    return pl.BlockSpec(shape, lambda *_: (0,) * nd, pipeline_mode=pl.Buffered(1))


def _rms_mod(x, g, scale, shift):
    inv = lax.rsqrt(jnp.mean(x * x, axis=-1, keepdims=True) + RMS_EPS)
    return (x * inv) * g * (1.0 + scale) + shift


def _mod_kernel(c_ref, w_ref, b_ref, o_ref):
    cv = c_ref[...]
    s = cv * jax.nn.sigmoid(cv)
    o_ref[...] = jnp.dot(s.astype(BF16), w_ref[...].astype(BF16),
                         preferred_element_type=F32) + b_ref[...]


def _modulation(cvec, w_mod, b_mod):
    n, d = cvec.shape
    width = w_mod.shape[1]
    tn = 1024
    return pl.pallas_call(
        _mod_kernel,
        grid=(width // tn,),
        in_specs=[pl.BlockSpec((n, d), lambda j: (0, 0)),
                  pl.BlockSpec((d, tn), lambda j: (0, j)),
                  pl.BlockSpec((1, tn), lambda j: (0, j))],
        out_specs=pl.BlockSpec((n, tn), lambda j: (0, j)),
        out_shape=jax.ShapeDtypeStruct((n, width), F32),
        compiler_params=pltpu.CompilerParams(dimension_semantics=("arbitrary",),
                                             vmem_limit_bytes=VMEM_LIMIT),
        name="modulation",
    )(cvec, w_mod, b_mod.reshape(1, width))


def _nat_variant_geometry(variant):
    if variant == 0:
        return 0, (lambda i, n: n < WIN_H)
    if variant == 1:
        return -(WIN_H // 2), (lambda i, n: 0 <= n - i < WIN_H)
    return -WIN_H, (lambda i, n: n >= NAT_K_ROWS - WIN_H)


def _bias_kernel(r_ref, o_ref):
    h = pl.program_id(0)
    n_dr = 2 * WIN_H - 1
    n_dc = 2 * WIN_W - 1
    qc = lax.broadcasted_iota(jnp.int32, (GRID_W, LANES), 0)
    lane = lax.broadcasted_iota(jnp.int32, (GRID_W, LANES), 1)
    kc = lane & (GRID_W - 1)
    hi_half = lane >= GRID_W
    dc = kc - qc + (WIN_W - 1)
    cstart = jnp.clip(qc - WIN_W // 2, 0, GRID_W - WIN_W)
    col_ok = (kc >= cstart) & (kc < cstart + WIN_W)
    neg = jnp.full((GRID_W, LANES), NEG_INF, F32)
    base = h * (n_dr * n_dc)
    tiles = []
    for a in range(n_dr):
        val = neg
        for b in range(n_dc):
            val = jnp.where(dc == b, r_ref[base + a * n_dc + b] * LOG2E, val)
        tiles.append(jnp.where(col_ok, val, neg))
    for variant in range(3):
        off, valid = _nat_variant_geometry(variant)
        for i in range(NAT_Q_ROWS):
            for p in range(NAT_K_ROWS // 2):
                halves = []
                for n in (2 * p, 2 * p + 1):
                    a = n - i + off + WIN_H - 1
                    halves.append(tiles[a] if valid(i, n) else neg)
                o_ref[variant, 0, i * GRID_W:(i + 1) * GRID_W, p * LANES:(p + 1) * LANES] = (
                    jnp.where(hi_half, halves[1], halves[0]))


def _expand_bias(nat_bias):
    heads = nat_bias.shape[0]
    nq = NAT_Q_ROWS * GRID_W
    nk = NAT_K_ROWS * GRID_W
    return pl.pallas_call(
        _bias_kernel,
        grid=(heads,),
        in_specs=[pl.BlockSpec(memory_space=pltpu.SMEM)],
        out_specs=pl.BlockSpec((3, 1, nq, nk), lambda h: (0, h, 0, 0)),
        out_shape=jax.ShapeDtypeStruct((3, heads, nq, nk), F32),
        compiler_params=pltpu.CompilerParams(dimension_semantics=("arbitrary",),
                                             vmem_limit_bytes=VMEM_LIMIT),
        name="bias_expand",
    )(nat_bias.reshape(-1))


def _pre_kernel(*refs, rope, emit_cache):
    x_ref, mod_ref, n1_ref, w_ref, qg_ref, kg_ref, hm_ref = refs[:7]
    pos = 7
    if rope:
        cos_ref, sin_ref = refs[pos:pos + 2]
        pos += 2
    qa_ref, ka_ref, va_ref, qb_ref, kb_ref, vb_ref = refs[pos:pos + 6]
    pos += 6
    if emit_cache:
        cak_ref, cav_ref, cbk_ref, cbv_ref = refs[pos:pos + 4]

    x = x_ref[0]
    mod = mod_ref[0]
    h = _rms_mod(x, n1_ref[...], mod[1:2], mod[0:1])
    acc = jnp.dot(h.astype(BF16), w_ref[...], preferred_element_type=F32)
    tm = x.shape[0]
    lane = lax.broadcasted_iota(jnp.int32, (tm, LANES), 1)
    lo_half = lane < HEAD_DIM
    hm = hm_ref[...]

    def head_norm(y, g):
        ss = y * y
        hi = ss.astype(BF16)
        lo = (ss - hi.astype(F32)).astype(BF16)
        ms = (jnp.dot(hi, hm, preferred_element_type=F32)
              + jnp.dot(lo, hm, preferred_element_type=F32)) * (1.0 / HEAD_DIM)
        return (y * lax.rsqrt(ms + RMS_EPS)) * g

    def rotary(y):
        partner = jnp.where((lane & ROPE_PAIRS) == 0,
                            pltpu.roll(y, LANES - ROPE_PAIRS, 1), pltpu.roll(y, ROPE_PAIRS, 1))
        return y * cos_ref[...] + partner * sin_ref[...]

    def store_dup(ref, y):
        r = pltpu.roll(y, HEAD_DIM, 1)
        ref[0, :, 0:LANES] = jnp.where(lo_half, y, r).astype(ref.dtype)
        ref[0, :, LANES:2 * LANES] = jnp.where(lo_half, r, y).astype(ref.dtype)

    for c in range(PAIRS):
        sl = slice(c * LANES, (c + 1) * LANES)
        y = head_norm(acc[:, sl], qg_ref[:, sl])
        if rope:
            y = rotary(y)
        qa_ref[0, :, sl] = y.astype(qa_ref.dtype)

    k = head_norm(acc[:, A_WIDTH:A_WIDTH + A_KV_WIDTH], kg_ref[...])
    v = acc[:, A_WIDTH + A_KV_WIDTH:A_WIDTH + 2 * A_KV_WIDTH]
    if emit_cache:
        cak_ref[0] = k
        cav_ref[0] = v
    if rope:
        k = rotary(k)
    store_dup(ka_ref, k)
    store_dup(va_ref, v)

    b0 = A_WIDTH + 2 * A_KV_WIDTH
    bq = acc[:, b0:b0 + B_WIDTH]
    bk = acc[:, b0 + B_WIDTH:b0 + 2 * B_WIDTH]
    bv = acc[:, b0 + 2 * B_WIDTH:b0 + 3 * B_WIDTH]
    qb_ref[0] = (bq * Q_SCALE).astype(qb_ref.dtype)
    kb_ref[0] = bk.astype(kb_ref.dtype)
    vb_ref[0] = bv.astype(vb_ref.dtype)
    if emit_cache:
        cbk_ref[0] = bk
        cbv_ref[0] = bv


def _pre_attention(x, mod, mod_row, n1, w_qkv, qg, kg, hm, rope_tables, emit_cache, tm):
    b, t, d = x.shape
    rope = rope_tables is not None
    tok = lambda w: pl.BlockSpec((1, tm, w), lambda i, j: (i, j, 0))
    in_specs = [tok(d),
                pl.BlockSpec((1, N_MOD, d), lambda i, j: (mod_row(i), 0, 0)),
                _const_spec((1, d)), _const_spec(w_qkv.shape), _const_spec(qg.shape),
                _const_spec(kg.shape), _const_spec(hm.shape)]
    args = [x, mod, n1, w_qkv, qg, kg, hm]
    if rope:
        in_specs += [pl.BlockSpec((tm, LANES), lambda i, j: (j, 0))] * 2
        args += list(rope_tables)
    widths = [A_WIDTH, 2 * LANES, 2 * LANES, B_WIDTH, B_WIDTH, B_WIDTH]
    out_specs = [tok(w) for w in widths]
    out_shape = [jax.ShapeDtypeStruct((b, t, w), BF16) for w in widths]
    if emit_cache:
        cw = [A_KV_WIDTH, A_KV_WIDTH, B_WIDTH, B_WIDTH]
        out_specs += [tok(w) for w in cw]
        out_shape += [jax.ShapeDtypeStruct((b, t, w), F32) for w in cw]
    return pl.pallas_call(
        functools.partial(_pre_kernel, rope=rope, emit_cache=emit_cache),
        grid=(b, t // tm),
        in_specs=in_specs, out_specs=out_specs, out_shape=out_shape,
        compiler_params=pltpu.CompilerParams(dimension_semantics=("arbitrary", "arbitrary"),
                                             vmem_limit_bytes=VMEM_LIMIT),
        name="pre_attention",
    )(*args)


def _flash_pairs(q, kv_sets, shared_kv, bias=None):
    tq, width = q.shape
    npairs = width // LANES
    lo_half = lax.broadcasted_iota(jnp.int32, (tq, LANES), 1) < HEAD_DIM
    zero = jnp.zeros((tq, LANES), q.dtype)
    q_tiles = [q[:, p * LANES:(p + 1) * LANES] for p in range(npairs)]
    groups = [(0, list(range(npairs)))] if shared_kv else [(p, [p]) for p in range(npairs)]
    stacked = [jnp.concatenate([jnp.where(lo_half, q_tiles[p], zero) for p in pairs]
                               + [jnp.where(lo_half, zero, q_tiles[p]) for p in pairs], axis=0)
               for _, pairs in groups]
    items = [(si, start, chunk, gi)
             for si, (_, _, n_keys, chunk) in enumerate(kv_sets)
             for start in range(0, n_keys, chunk)
             for gi in range(len(groups))]

    def scores(item):
        si, start, size, gi = item
        k = kv_sets[si][0](groups[gi][0], start, size)
        s = lax.dot_general(stacked[gi], k, (((1,), (1,)), ((), ())), preferred_element_type=F32)
        if bias is not None and si == bias[0]:
            s = s + bias[1](gi)
        return s

    carries = [None] * len(groups)
    s_next = scores(items[0])
    for idx, item in enumerate(items):
        s = s_next
        if idx + 1 < len(items):
            s_next = scores(items[idx + 1])
        si, start, size, gi = item
        v = kv_sets[si][1](groups[gi][0], start, size)
        lo_k = lax.broadcasted_iota(jnp.int32, v.shape, 1) < HEAD_DIM
        one = jnp.ones_like(v)
        v_lo = jnp.where(lo_k, v, one)
        v_hi = jnp.where(lo_k, one, v)
        half = s.shape[0] // 2
        m_cur = jnp.max(s, axis=1, keepdims=True)
        if carries[gi] is None:
            m_new = m_cur
            p = jnp.exp2(s - m_new).astype(BF16)
            acc_lo = jnp.dot(p[:half], v_lo, preferred_element_type=F32)
            acc_hi = jnp.dot(p[half:], v_hi, preferred_element_type=F32)
        else:
            m_prev, acc_lo, acc_hi = carries[gi]
            m_new = jnp.maximum(m_prev, m_cur)
            alpha = jnp.exp2(m_prev - m_new)
            p = jnp.exp2(s - m_new).astype(BF16)
            acc_lo = acc_lo * alpha[:half] + jnp.dot(p[:half], v_lo, preferred_element_type=F32)
            acc_hi = acc_hi * alpha[half:] + jnp.dot(p[half:], v_hi, preferred_element_type=F32)
        carries[gi] = (m_new, acc_lo, acc_hi)

    outs = [None] * npairs
    for gi, (_, pairs) in enumerate(groups):
        _, acc_lo, acc_hi = carries[gi]
        lo_rows = lax.broadcasted_iota(jnp.int32, acc_lo.shape, 1) < HEAD_DIM
        r_lo = acc_lo / jnp.where(lo_rows, pltpu.roll(acc_lo, HEAD_DIM, 1), 1.0)
        r_hi = acc_hi / jnp.where(lo_rows, 1.0, pltpu.roll(acc_hi, HEAD_DIM, 1))
        for i, p in enumerate(pairs):
            outs[p] = jnp.where(lo_half, r_lo[i * tq:(i + 1) * tq], r_hi[i * tq:(i + 1) * tq])
    return outs


def _ref_loader(ref):
    return lambda tile, start, size: ref[0, start:start + size, tile * LANES:(tile + 1) * LANES]


def _store_pairs(o_ref, outs):
    for p, o in enumerate(outs):
        o_ref[0, :, p * LANES:(p + 1) * LANES] = o.astype(o_ref.dtype)


def _attn_full_kernel(q_ref, *refs, chunks, shared_kv):
    o_ref = refs[-1]
    kv = refs[:-1]
    kv_sets = [(_ref_loader(kv[2 * i]), _ref_loader(kv[2 * i + 1]), kv[2 * i].shape[1], chunks[i])
               for i in range(len(chunks))]
    _store_pairs(o_ref, _flash_pairs(q_ref[0], kv_sets, shared_kv))


def _full_attention(q, kv_sets, shared_kv, pairs_per_step, tq, chunks):
    b, t, _ = q.shape
    qspec = pl.BlockSpec((1, tq, pairs_per_step * LANES), lambda i, c, j: (i, j, c))
    kv_width = LANES if shared_kv else pairs_per_step * LANES
    in_specs = [qspec]
    args = [q]
    for k, v in kv_sets:
        spec = pl.BlockSpec((1, k.shape[1], kv_width), lambda i, c, j: (i, 0, c))
        in_specs += [spec, spec]
        args += [k, v]
    return pl.pallas_call(
        functools.partial(_attn_full_kernel, chunks=tuple(chunks), shared_kv=shared_kv),
        grid=(b, PAIRS // pairs_per_step, t // tq),
        in_specs=in_specs, out_specs=qspec,
        out_shape=jax.ShapeDtypeStruct(q.shape, BF16),
        compiler_params=pltpu.CompilerParams(dimension_semantics=("arbitrary",) * 3,
                                             vmem_limit_bytes=VMEM_LIMIT),
        name="full_attention",
    )(*args)


NAT_PAIRS_PER_STEP = 2


def _attn_nat_kernel(q_ref, kw_ref, vw_ref, kc_ref, vc_ref, bias_ref, o_ref, *, rows):
    j = pl.program_id(2)
    ws = jnp.clip(j * NAT_Q_ROWS - WIN_H // 2, 0, rows - NAT_K_ROWS)
    win = pl.multiple_of(ws * GRID_W, GRID_W)
    nk = NAT_K_ROWS * GRID_W
    tq = q_ref.shape[1]

    def window(ref):
        return lambda tile, start, size: ref[0, pl.ds(win, nk), tile * LANES:(tile + 1) * LANES]

    kv_sets = [(_ref_loader(kc_ref), _ref_loader(vc_ref), kc_ref.shape[1], kc_ref.shape[1]),
               (window(kw_ref), window(vw_ref), nk, nk)]
    load_bias = lambda gi: bias_ref[0, 2 * gi:2 * gi + 2].reshape(2 * tq, nk)
    _store_pairs(o_ref, _flash_pairs(q_ref[0], kv_sets, False, bias=(1, load_bias)))


def _nat_attention(q, k, v, ctx_k, ctx_v, bias):
    b, t, _ = q.shape
    rows = t // GRID_W
    nblk = rows // NAT_Q_ROWS
    tq = NAT_Q_ROWS * GRID_W
    nk = NAT_K_ROWS * GRID_W
    width = NAT_PAIRS_PER_STEP * LANES
    qspec = pl.BlockSpec((1, tq, width), lambda i, c, j: (i, j, c))
    whole = lambda n: pl.BlockSpec((1, n, width), lambda i, c, j: (i, 0, c))

    def bias_map(i, c, j):
        variant = jnp.where(j == 0, 0, jnp.where(j == nblk - 1, 2, 1))
        return (variant, c, 0, 0)

    return pl.pallas_call(
        functools.partial(_attn_nat_kernel, rows=rows),
        grid=(b, PAIRS // NAT_PAIRS_PER_STEP, nblk),
        in_specs=[qspec, whole(t), whole(t), whole(ctx_k.shape[1]), whole(ctx_v.shape[1]),
                  pl.BlockSpec((1, 2 * NAT_PAIRS_PER_STEP, tq, nk), bias_map)],
        out_specs=qspec,
        out_shape=jax.ShapeDtypeStruct(q.shape, BF16),
        compiler_params=pltpu.CompilerParams(dimension_semantics=("arbitrary",) * 3,
                                             vmem_limit_bytes=VMEM_LIMIT),
        name="nat_attention",
    )(q, k, v, ctx_k, ctx_v, bias)


def _post_kernel(x_ref, ao_ref, bo_ref, mod_ref, n1_ref, n2_ref, fg_ref,
                 wg_ref, wa_ref, wb_ref, wo_ref, w1_ref, w2_ref, y_ref):
    x = x_ref[0]
    mod = mod_ref[0]
    h = _rms_mod(x, n1_ref[...], mod[1:2], mod[0:1])
    gates = jnp.dot(h.astype(BF16), wg_ref[...], preferred_element_type=F32)
    ma = jnp.dot(ao_ref[0], wa_ref[...], preferred_element_type=F32)
    mb = jnp.dot(bo_ref[0], wb_ref[...], preferred_element_type=F32)
    m = jax.nn.sigmoid(gates[:, :D_MODEL]) * ma + jax.nn.sigmoid(gates[:, D_MODEL:]) * mb
    x1 = x + mod[2:3] * jnp.dot(m.astype(BF16), wo_ref[...], preferred_element_type=F32)
    h2 = _rms_mod(x1, n2_ref[...], mod[4:5], mod[3:4])
    u = jnp.maximum(jnp.dot(h2.astype(BF16), w1_ref[...], preferred_element_type=F32), 0.0)
    x2 = x1 + mod[5:6] * jnp.dot((u * u).astype(BF16), w2_ref[...], preferred_element_type=F32)
    inv = lax.rsqrt(jnp.mean(x2 * x2, axis=-1, keepdims=True) + RMS_EPS)
    y_ref[0] = (x2 * inv) * fg_ref[...]


def _post_attention(x, a_o, b_o, mod, mod_row, n1, n2, fg, wg, wa, wb, wo, w1, w2, tm):
    b, t, d = x.shape
    tok = lambda w: pl.BlockSpec((1, tm, w), lambda i, j: (i, j, 0))
    weights = [wg, wa, wb, wo, w1, w2]
    return pl.pallas_call(
        _post_kernel,
        grid=(b, t // tm),
        in_specs=[tok(d), tok(A_WIDTH), tok(B_WIDTH),
                  pl.BlockSpec((1, N_MOD, d), lambda i, j: (mod_row(i), 0, 0)),
                  _const_spec((1, d)), _const_spec((1, d)), _const_spec((1, d))]
                 + [_const_spec(w.shape) for w in weights],
        out_specs=tok(d),
        out_shape=jax.ShapeDtypeStruct(x.shape, F32),
        compiler_params=pltpu.CompilerParams(dimension_semantics=("arbitrary", "arbitrary"),
                                             vmem_limit_bytes=VMEM_LIMIT),
        name="post_attention",
    )(x, a_o, b_o, mod, n1, n2, fg, *weights)


def _rope_tables(t):
    pos = jnp.arange(t, dtype=jnp.int32)
    row = (pos // GRID_W).astype(F32)
    col = (pos % GRID_W).astype(F32)
    inv = ROPE_BASE ** (-jnp.arange(ROPE_PAIRS, dtype=F32) / ROPE_PAIRS)
    ar = row[:, None] * inv
    ac = col[:, None] * inv
    cos = jnp.concatenate([jnp.cos(ar), jnp.cos(ar), jnp.cos(ac), jnp.cos(ac)], axis=-1)
    sin = jnp.concatenate([-jnp.sin(ar), jnp.sin(ar), -jnp.sin(ac), jnp.sin(ac)], axis=-1)
    return jnp.tile(cos, (1, LANES // HEAD_DIM)), jnp.tile(sin, (1, LANES // HEAD_DIM))


def _dup_heads(x):
    b, t, kv, dh = x.shape
    return jnp.broadcast_to(x[:, :, :, None, :], (b, t, kv, 2, dh)).reshape(b, t, kv * 2 * dh).astype(BF16)


def kernel(x_prompt, x_sample, cache_a_k, cache_a_v, cache_b_k, cache_b_v, c, c_ctx, w_mod, b_mod,
           norm1_g, norm2_g, w_in, q_norm_g, k_norm_g, nat_bias, w_br_a, w_br_b, w_out, w_mlp_in,
           w_mlp_out, final_norm_g):
    assert w_mod.shape[0] == 1, "single-layer trunk"
    nb, seq, d = x_prompt.shape
    nd, dseq, _ = x_sample.shape
    past = cache_a_k.shape[2]

    ctx_row = nd
    pad = (-(nd + 1)) % 8
    cvec = jnp.concatenate([c, c_ctx[None, :], jnp.zeros((pad, d), F32)], axis=0)
    mod = _modulation(cvec, w_mod[0], b_mod[0]).reshape(cvec.shape[0], N_MOD, d)

    w_in_b = w_in[0].astype(BF16)
    w_qkv = w_in_b[:, :QKV_WIDTH]
    w_gate = w_in_b[:, QKV_WIDTH:]
    wa = w_br_a[0].astype(BF16)
    wb = w_br_b[0].astype(BF16)
    wo = w_out[0].astype(BF16)
    w1 = w_mlp_in[0].astype(BF16)
    w2 = w_mlp_out[0].astype(BF16)
    n1 = norm1_g[0].reshape(1, d)
    n2 = norm2_g[0].reshape(1, d)
    fg = final_norm_g.reshape(1, d)
    qg = (jnp.tile(q_norm_g[0], A_HEADS) * Q_SCALE).reshape(1, A_WIDTH)
    kg = jnp.tile(k_norm_g[0], A_KV_HEADS).reshape(1, A_KV_WIDTH)
    head_id = np.arange(LANES) // HEAD_DIM
    hm = jnp.asarray(head_id[:, None] == head_id[None, :], dtype=BF16)

    bias = _expand_bias(nat_bias[0])

    ctx_mod_row = lambda i: ctx_row
    (qa, ka, va, qb, kb, vb, new_ak, new_av, new_bk, new_bv) = _pre_attention(
        x_prompt, mod, ctx_mod_row, n1, w_qkv, qg, kg, hm, None, True, seq)
    a_o = _full_attention(qa, [(ka, va)], True, 2, seq, [seq])
    b_o = _full_attention(qb, [(kb, vb)], False, PAIRS, seq, [seq])
    y_prompt = _post_attention(x_prompt, a_o, b_o, mod, ctx_mod_row, n1, n2, fg,
                               w_gate, wa, wb, wo, w1, w2, seq)

    lat_mod_row = lambda i: i
    (qa, ka, va, qb, kb, vb) = _pre_attention(
        x_sample, mod, lat_mod_row, n1, w_qkv, qg, kg, hm, _rope_tables(dseq), False, 256)
    ca_k = _dup_heads(cache_a_k[:, 0])
    ca_v = _dup_heads(cache_a_v[:, 0])
    cb_k = cache_b_k[:, 0].reshape(nd, past, B_WIDTH).astype(BF16)
    cb_v = cache_b_v[:, 0].reshape(nd, past, B_WIDTH).astype(BF16)
    a_o = _full_attention(qa, [(ca_k, ca_v), (ka, va)], True, 2, 256, [past, 1024])
    b_o = _nat_attention(qb, kb, vb, cb_k, cb_v, bias)
    y_sample = _post_attention(x_sample, a_o, b_o, mod, lat_mod_row, n1, n2, fg,
                               w_gate, wa, wb, wo, w1, w2, 256)

    return (y_prompt, y_sample,
            new_ak.reshape(nb, 1, seq, A_KV_HEADS, HEAD_DIM),
            new_av.reshape(nb, 1, seq, A_KV_HEADS, HEAD_DIM),
            new_bk.reshape(nb, 1, seq, B_HEADS, HEAD_DIM),
            new_bv.reshape(nb, 1, seq, B_HEADS, HEAD_DIM))
```

```python
import functools

import numpy as np
import jax
import jax.numpy as jnp
from jax import lax
from jax.experimental import pallas as pl
from jax.experimental.pallas import tpu as pltpu

F32 = jnp.float32
BF16 = jnp.bfloat16

D_MODEL = 1024
GRID_W = 64
HEAD_DIM = 64
A_HEADS = 8
A_KV_HEADS = 2
B_HEADS = 8
A_WIDTH = A_HEADS * HEAD_DIM
A_KV_WIDTH = A_KV_HEADS * HEAD_DIM
B_WIDTH = B_HEADS * HEAD_DIM
D_FF = 4 * D_MODEL
WIN_H = 8
WIN_W = 16
ROPE_BASE = 10000.0
ROPE_PAIRS = HEAD_DIM // 4
RMS_EPS = 1e-6
N_MOD = 6
NEG_INF = -1e30

LANES = 128
PAIRS = A_WIDTH // LANES
QKV_WIDTH = A_WIDTH + 2 * A_KV_WIDTH + 3 * B_WIDTH
GATE_WIDTH = 2 * D_MODEL
NAT_Q_ROWS = 4
NAT_K_ROWS = 12
VMEM_LIMIT = 56 * 1024 * 1024
LOG2E = 1.4426950408889634
Q_SCALE = HEAD_DIM ** -0.5 * LOG2E


def _const_spec(shape):
    nd = len(shape)
    return pl.BlockSpec(shape, lambda *_: (0,) * nd, pipeline_mode=pl.Buffered(1))


def _rms_mod(x, g, scale, shift):
    inv = lax.rsqrt(jnp.mean(x * x, axis=-1, keepdims=True) + RMS_EPS)
    return (x * inv) * g * (1.0 + scale) + shift


def _mod_kernel(c_ref, w_ref, b_ref, o_ref):
    cv = c_ref[...]
    s = cv * jax.nn.sigmoid(cv)
    o_ref[...] = jnp.dot(s.astype(BF16), w_ref[...].astype(BF16),
                         preferred_element_type=F32) + b_ref[...]


def _modulation(cvec, w_mod, b_mod):
    n, d = cvec.shape
    width = w_mod.shape[1]
    tn = 1024
    return pl.pallas_call(
        _mod_kernel,
        grid=(width // tn,),
        in_specs=[pl.BlockSpec((n, d), lambda j: (0, 0)),
                  pl.BlockSpec((d, tn), lambda j: (0, j)),
                  pl.BlockSpec((1, tn), lambda j: (0, j))],
        out_specs=pl.BlockSpec((n, tn), lambda j: (0, j)),
        out_shape=jax.ShapeDtypeStruct((n, width), F32),
        compiler_params=pltpu.CompilerParams(dimension_semantics=("arbitrary",),
                                             vmem_limit_bytes=VMEM_LIMIT),
        name="modulation",
    )(cvec, w_mod, b_mod.reshape(1, width))


def _nat_variant_geometry(variant):
    if variant == 0:
        return 0, (lambda i, n: n < WIN_H)
    if variant == 1:
        return -(WIN_H // 2), (lambda i, n: 0 <= n - i < WIN_H)
    return -WIN_H, (lambda i, n: n >= NAT_K_ROWS - WIN_H)


def _bias_kernel(r_ref, o_ref):
    h = pl.program_id(0)
    n_dr = 2 * WIN_H - 1
    n_dc = 2 * WIN_W - 1
    qc = lax.broadcasted_iota(jnp.int32, (GRID_W, LANES), 0)
    lane = lax.broadcasted_iota(jnp.int32, (GRID_W, LANES), 1)
    kc = lane & (GRID_W - 1)
    hi_half = lane >= GRID_W
    dc = kc - qc + (WIN_W - 1)
    cstart = jnp.clip(qc - WIN_W // 2, 0, GRID_W - WIN_W)
    col_ok = (kc >= cstart) & (kc < cstart + WIN_W)
    neg = jnp.full((GRID_W, LANES), NEG_INF, F32)
    base = h * (n_dr * n_dc)
    tiles = []
    for a in range(n_dr):
        val = neg
        for b in range(n_dc):
            val = jnp.where(dc == b, r_ref[base + a * n_dc + b] * LOG2E, val)
        tiles.append(jnp.where(col_ok, val, neg))
    for variant in range(3):
        off, valid = _nat_variant_geometry(variant)
        for i in range(NAT_Q_ROWS):
            for p in range(NAT_K_ROWS // 2):
                halves = []
                for n in (2 * p, 2 * p + 1):
                    a = n - i + off + WIN_H - 1
                    halves.append(tiles[a] if valid(i, n) else neg)
                o_ref[variant, 0, i * GRID_W:(i + 1) * GRID_W, p * LANES:(p + 1) * LANES] = (
                    jnp.where(hi_half, halves[1], halves[0]))


def _expand_bias(nat_bias):
    heads = nat_bias.shape[0]
    nq = NAT_Q_ROWS * GRID_W
    nk = NAT_K_ROWS * GRID_W
    return pl.pallas_call(
        _bias_kernel,
        grid=(heads,),
        in_specs=[pl.BlockSpec(memory_space=pltpu.SMEM)],
        out_specs=pl.BlockSpec((3, 1, nq, nk), lambda h: (0, h, 0, 0)),
        out_shape=jax.ShapeDtypeStruct((3, heads, nq, nk), F32),
        compiler_params=pltpu.CompilerParams(dimension_semantics=("arbitrary",),
                                             vmem_limit_bytes=VMEM_LIMIT),
        name="bias_expand",
    )(nat_bias.reshape(-1))


def _pre_kernel(*refs, rope, emit_cache):
    x_ref, mod_ref, n1_ref, w_ref, qg_ref, kg_ref, hm_ref = refs[:7]
    pos = 7
    if rope:
        cos_ref, sin_ref = refs[pos:pos + 2]
        pos += 2
    qa_ref, ka_ref, va_ref, qb_ref, kb_ref, vb_ref = refs[pos:pos + 6]
    pos += 6
    if emit_cache:
        cak_ref, cav_ref, cbk_ref, cbv_ref = refs[pos:pos + 4]

    x = x_ref[0]
    mod = mod_ref[0]
    h = _rms_mod(x, n1_ref[...], mod[1:2], mod[0:1])
    acc = jnp.dot(h.astype(BF16), w_ref[...], preferred_element_type=F32)
    tm = x.shape[0]
    lane = lax.broadcasted_iota(jnp.int32, (tm, LANES), 1)
    lo_half = lane < HEAD_DIM
    hm = hm_ref[...]

    def head_norm(y, g):
        ms = jnp.dot((y * y).astype(BF16), hm, preferred_element_type=F32) * (1.0 / HEAD_DIM)
        return (y * lax.rsqrt(ms + RMS_EPS)) * g

    def rotary(y):
        partner = jnp.where((lane & ROPE_PAIRS) == 0,
                            pltpu.roll(y, LANES - ROPE_PAIRS, 1), pltpu.roll(y, ROPE_PAIRS, 1))
        return y * cos_ref[...] + partner * sin_ref[...]

    def store_dup(ref, y):
        r = pltpu.roll(y, HEAD_DIM, 1)
        ref[0, :, 0:LANES] = jnp.where(lo_half, y, r).astype(ref.dtype)
        ref[0, :, LANES:2 * LANES] = jnp.where(lo_half, r, y).astype(ref.dtype)

    for c in range(PAIRS):
        sl = slice(c * LANES, (c + 1) * LANES)
        y = head_norm(acc[:, sl], qg_ref[:, sl])
        if rope:
            y = rotary(y)
        qa_ref[0, :, sl] = y.astype(qa_ref.dtype)

    k = head_norm(acc[:, A_WIDTH:A_WIDTH + A_KV_WIDTH], kg_ref[...])
    v = acc[:, A_WIDTH + A_KV_WIDTH:A_WIDTH + 2 * A_KV_WIDTH]
    if emit_cache:
        cak_ref[0] = k
        cav_ref[0] = v
    if rope:
        k = rotary(k)
    store_dup(ka_ref, k)
    vt = v.T
    for g in range(A_KV_HEADS):
        va_ref[0, g] = vt[g * HEAD_DIM:(g + 1) * HEAD_DIM].astype(va_ref.dtype)

    b0 = A_WIDTH + 2 * A_KV_WIDTH
    bq = acc[:, b0:b0 + B_WIDTH]
    bk = acc[:, b0 + B_WIDTH:b0 + 2 * B_WIDTH]
    bv = acc[:, b0 + 2 * B_WIDTH:b0 + 3 * B_WIDTH]
    qb_ref[0] = (bq * Q_SCALE).astype(qb_ref.dtype)
    kb_ref[0] = bk.astype(kb_ref.dtype)
    vb_ref[0] = bv.astype(vb_ref.dtype)
    if emit_cache:
        cbk_ref[0] = bk
        cbv_ref[0] = bv


def _pre_attention(x, mod, mod_row, n1, w_qkv, qg, kg, hm, rope_tables, emit_cache, tm):
    b, t, d = x.shape
    rope = rope_tables is not None
    tok = lambda w: pl.BlockSpec((1, tm, w), lambda i, j: (i, j, 0))
    in_specs = [tok(d),
                pl.BlockSpec((1, N_MOD, d), lambda i, j: (mod_row(i), 0, 0)),
                _const_spec((1, d)), _const_spec(w_qkv.shape), _const_spec(qg.shape),
                _const_spec(kg.shape), _const_spec(hm.shape)]
    args = [x, mod, n1, w_qkv, qg, kg, hm]
    if rope:
        in_specs += [pl.BlockSpec((tm, LANES), lambda i, j: (j, 0))] * 2
        args += list(rope_tables)
    widths = [A_WIDTH, 2 * LANES, None, B_WIDTH, B_WIDTH, B_WIDTH]
    vt_spec = pl.BlockSpec((1, A_KV_HEADS, HEAD_DIM, tm), lambda i, j: (i, 0, 0, j))
    out_specs = [vt_spec if w is None else tok(w) for w in widths]
    out_shape = [jax.ShapeDtypeStruct((b, A_KV_HEADS, HEAD_DIM, t) if w is None else (b, t, w), BF16)
                 for w in widths]
    if emit_cache:
        cw = [A_KV_WIDTH, A_KV_WIDTH, B_WIDTH, B_WIDTH]
        out_specs += [tok(w) for w in cw]
        out_shape += [jax.ShapeDtypeStruct((b, t, w), F32) for w in cw]
    return pl.pallas_call(
        functools.partial(_pre_kernel, rope=rope, emit_cache=emit_cache),
        grid=(b, t // tm),
        in_specs=in_specs, out_specs=out_specs, out_shape=out_shape,
        compiler_params=pltpu.CompilerParams(dimension_semantics=("arbitrary", "arbitrary"),
                                             vmem_limit_bytes=VMEM_LIMIT),
        name="pre_attention",
    )(*args)


def _flash_pairs(q, kv_sets, bias=None):
    tq, width = q.shape
    npairs = width // LANES
    lo_half = lax.broadcasted_iota(jnp.int32, (tq, LANES), 1) < HEAD_DIM
    zero = jnp.zeros((tq, LANES), q.dtype)
    q_tiles = [q[:, p * LANES:(p + 1) * LANES] for p in range(npairs)]
    stacked = [jnp.concatenate([jnp.where(lo_half, t, zero), jnp.where(lo_half, zero, t)], axis=0)
               for t in q_tiles]
    items = [(si, start, chunk, pair)
             for si, (_, _, n_keys, chunk) in enumerate(kv_sets)
             for start in range(0, n_keys, chunk)
             for pair in range(npairs)]

    def scores(item):
        si, start, size, pair = item
        k = kv_sets[si][0](pair, start, size)
        s = lax.dot_general(stacked[pair], k, (((1,), (1,)), ((), ())), preferred_element_type=F32)
        if bias is not None and si == bias[0]:
            s = s + bias[1](pair)
        return s

    carries = [None] * npairs
    s_next = scores(items[0])
    for idx, item in enumerate(items):
        s = s_next
        if idx + 1 < len(items):
            s_next = scores(items[idx + 1])
        si, start, size, pair = item
        v = kv_sets[si][1](pair, start, size)
        lo_k = lax.broadcasted_iota(jnp.int32, v.shape, 1) < HEAD_DIM
        one = jnp.ones_like(v)
        v_lo = jnp.where(lo_k, v, one)
        v_hi = jnp.where(lo_k, one, v)
        m_cur = jnp.max(s, axis=1, keepdims=True)
        if carries[pair] is None:
            m_new = m_cur
            p = jnp.exp2(s - m_new).astype(BF16)
            acc_lo = jnp.dot(p[:tq], v_lo, preferred_element_type=F32)
            acc_hi = jnp.dot(p[tq:], v_hi, preferred_element_type=F32)
        else:
            m_prev, acc_lo, acc_hi = carries[pair]
            m_new = jnp.maximum(m_prev, m_cur)
            alpha = jnp.exp2(m_prev - m_new)
            p = jnp.exp2(s - m_new).astype(BF16)
            acc_lo = acc_lo * alpha[:tq] + jnp.dot(p[:tq], v_lo, preferred_element_type=F32)
            acc_hi = acc_hi * alpha[tq:] + jnp.dot(p[tq:], v_hi, preferred_element_type=F32)
        carries[pair] = (m_new, acc_lo, acc_hi)

    outs = []
    for _, acc_lo, acc_hi in carries:
        r_lo = acc_lo / jnp.where(lo_half, pltpu.roll(acc_lo, HEAD_DIM, 1), 1.0)
        r_hi = acc_hi / jnp.where(lo_half, 1.0, pltpu.roll(acc_hi, HEAD_DIM, 1))
        outs.append(jnp.where(lo_half, r_lo, r_hi))
    return outs


ONES_ROWS = 16


def _flash_shared_t(q, kv_sets):
    tq, width = q.shape
    npairs = width // LANES
    lo_half = lax.broadcasted_iota(jnp.int32, (tq, LANES), 1) < HEAD_DIM
    zero = jnp.zeros((tq, LANES), q.dtype)
    q_tiles = [q[:, p * LANES:(p + 1) * LANES] for p in range(npairs)]
    stacked = jnp.concatenate([jnp.where(lo_half, t, zero) for t in q_tiles]
                              + [jnp.where(lo_half, zero, t) for t in q_tiles], axis=0)
    items = [(si, start, chunk)
             for si, (_, _, n_keys, chunk) in enumerate(kv_sets)
             for start in range(0, n_keys, chunk)]

    def scores(item):
        si, start, size = item
        k = kv_sets[si][0](start, size)
        return lax.dot_general(k, stacked, (((1,), (1,)), ((), ())), preferred_element_type=F32)

    carry = None
    s_next = scores(items[0])
    for idx, (si, start, size) in enumerate(items):
        s = s_next
        if idx + 1 < len(items):
            s_next = scores(items[idx + 1])
        vt = kv_sets[si][1](start, size)
        v_aug = jnp.concatenate([vt, jnp.ones((ONES_ROWS, size), vt.dtype)], axis=0)
        m_cur = jnp.max(s, axis=0, keepdims=True)
        if carry is None:
            m_new = m_cur
            acc = jnp.dot(v_aug, jnp.exp2(s - m_new).astype(BF16), preferred_element_type=F32)
        else:
            m_prev, acc = carry
            m_new = jnp.maximum(m_prev, m_cur)
            acc = acc * jnp.exp2(m_prev - m_new) + jnp.dot(
                v_aug, jnp.exp2(s - m_new).astype(BF16), preferred_element_type=F32)
        carry = (m_new, acc)

    acc = carry[1]
    o_t = acc[:HEAD_DIM] / acc[HEAD_DIM:HEAD_DIM + 1]
    outs = []
    for p in range(npairs):
        pair_t = jnp.concatenate([o_t[:, p * tq:(p + 1) * tq],
                                  o_t[:, (npairs + p) * tq:(npairs + p + 1) * tq]], axis=0)
        outs.append(pair_t.T)
    return outs


def _attn_shared_kernel(q_ref, *refs, chunks):
    o_ref = refs[-1]
    kv = refs[:-1]
    kv_sets = []
    for i, chunk in enumerate(chunks):
        k_ref, vt_ref = kv[2 * i], kv[2 * i + 1]
        kv_sets.append((lambda start, size, r=k_ref: r[0, start:start + size, :],
                        lambda start, size, r=vt_ref: r[0, 0, :, start:start + size],
                        k_ref.shape[1], chunk))
    _store_pairs(o_ref, _flash_shared_t(q_ref[0], kv_sets))


def _shared_attention(q, kv_sets, tq, chunks):
    b, t, _ = q.shape
    pairs_per_step = PAIRS // A_KV_HEADS
    qspec = pl.BlockSpec((1, tq, pairs_per_step * LANES), lambda i, c, j: (i, j, c))
    in_specs = [qspec]
    args = [q]
    for k, vt in kv_sets:
        nk = k.shape[1]
        in_specs += [pl.BlockSpec((1, nk, LANES), lambda i, c, j: (i, 0, c)),
                     pl.BlockSpec((1, 1, HEAD_DIM, nk), lambda i, c, j: (i, c, 0, 0))]
        args += [k, vt]
    return pl.pallas_call(
        functools.partial(_attn_shared_kernel, chunks=tuple(chunks)),
        grid=(b, A_KV_HEADS, t // tq),
        in_specs=in_specs, out_specs=qspec,
        out_shape=jax.ShapeDtypeStruct(q.shape, BF16),
        compiler_params=pltpu.CompilerParams(dimension_semantics=("arbitrary",) * 3,
                                             vmem_limit_bytes=VMEM_LIMIT),
        name="shared_attention",
    )(*args)


def _ref_loader(ref):
    return lambda tile, start, size: ref[0, start:start + size, tile * LANES:(tile + 1) * LANES]


def _store_pairs(o_ref, outs):
    for p, o in enumerate(outs):
        o_ref[0, :, p * LANES:(p + 1) * LANES] = o.astype(o_ref.dtype)


def _attn_full_kernel(q_ref, *refs, chunks):
    o_ref = refs[-1]
    kv = refs[:-1]
    kv_sets = [(_ref_loader(kv[2 * i]), _ref_loader(kv[2 * i + 1]), kv[2 * i].shape[1], chunks[i])
               for i in range(len(chunks))]
    _store_pairs(o_ref, _flash_pairs(q_ref[0], kv_sets))


def _full_attention(q, kv_sets, pairs_per_step, tq, chunks):
    b, t, _ = q.shape
    qspec = pl.BlockSpec((1, tq, pairs_per_step * LANES), lambda i, c, j: (i, j, c))
    in_specs = [qspec]
    args = [q]
    for k, v in kv_sets:
        spec = pl.BlockSpec((1, k.shape[1], pairs_per_step * LANES), lambda i, c, j: (i, 0, c))
        in_specs += [spec, spec]
        args += [k, v]
    return pl.pallas_call(
        functools.partial(_attn_full_kernel, chunks=tuple(chunks)),
        grid=(b, PAIRS // pairs_per_step, t // tq),
        in_specs=in_specs, out_specs=qspec,
        out_shape=jax.ShapeDtypeStruct(q.shape, BF16),
        compiler_params=pltpu.CompilerParams(dimension_semantics=("arbitrary",) * 3,
                                             vmem_limit_bytes=VMEM_LIMIT),
        name="full_attention",
    )(*args)


NAT_PAIRS_PER_STEP = 2


def _attn_nat_kernel(q_ref, kw_ref, vw_ref, kc_ref, vc_ref, bias_ref, o_ref, *, rows):
    j = pl.program_id(2)
    ws = jnp.clip(j * NAT_Q_ROWS - WIN_H // 2, 0, rows - NAT_K_ROWS)
    win = pl.multiple_of(ws * GRID_W, GRID_W)
    nk = NAT_K_ROWS * GRID_W
    tq = q_ref.shape[1]

    def window(ref):
        return lambda tile, start, size: ref[0, pl.ds(win, nk), tile * LANES:(tile + 1) * LANES]

    kv_sets = [(_ref_loader(kc_ref), _ref_loader(vc_ref), kc_ref.shape[1], kc_ref.shape[1]),
               (window(kw_ref), window(vw_ref), nk, nk)]
    load_bias = lambda gi: bias_ref[0, 2 * gi:2 * gi + 2].reshape(2 * tq, nk)
    _store_pairs(o_ref, _flash_pairs(q_ref[0], kv_sets, bias=(1, load_bias)))


def _nat_attention(q, k, v, ctx_k, ctx_v, bias):
    b, t, _ = q.shape
    rows = t // GRID_W
    nblk = rows // NAT_Q_ROWS
    tq = NAT_Q_ROWS * GRID_W
    nk = NAT_K_ROWS * GRID_W
    width = NAT_PAIRS_PER_STEP * LANES
    qspec = pl.BlockSpec((1, tq, width), lambda i, c, j: (i, j, c))
    whole = lambda n: pl.BlockSpec((1, n, width), lambda i, c, j: (i, 0, c))

    def bias_map(i, c, j):
        variant = jnp.where(j == 0, 0, jnp.where(j == nblk - 1, 2, 1))
        return (variant, c, 0, 0)

    return pl.pallas_call(
        functools.partial(_attn_nat_kernel, rows=rows),
        grid=(b, PAIRS // NAT_PAIRS_PER_STEP, nblk),
        in_specs=[qspec, whole(t), whole(t), whole(ctx_k.shape[1]), whole(ctx_v.shape[1]),
                  pl.BlockSpec((1, 2 * NAT_PAIRS_PER_STEP, tq, nk), bias_map)],
        out_specs=qspec,
        out_shape=jax.ShapeDtypeStruct(q.shape, BF16),
        compiler_params=pltpu.CompilerParams(dimension_semantics=("arbitrary",) * 3,
                                             vmem_limit_bytes=VMEM_LIMIT),
        name="nat_attention",
    )(q, k, v, ctx_k, ctx_v, bias)


def _post_kernel(x_ref, ao_ref, bo_ref, mod_ref, n1_ref, n2_ref, fg_ref,
                 wg_ref, wa_ref, wb_ref, wo_ref, w1_ref, w2_ref, y_ref):
    x = x_ref[0]
    mod = mod_ref[0]
    h = _rms_mod(x, n1_ref[...], mod[1:2], mod[0:1])
    gates = jnp.dot(h.astype(BF16), wg_ref[...], preferred_element_type=F32)
    ma = jnp.dot(ao_ref[0], wa_ref[...], preferred_element_type=F32)
    mb = jnp.dot(bo_ref[0], wb_ref[...], preferred_element_type=F32)
    m = jax.nn.sigmoid(gates[:, :D_MODEL]) * ma + jax.nn.sigmoid(gates[:, D_MODEL:]) * mb
    x1 = x + mod[2:3] * jnp.dot(m.astype(BF16), wo_ref[...], preferred_element_type=F32)
    h2 = _rms_mod(x1, n2_ref[...], mod[4:5], mod[3:4])
    u = jnp.maximum(jnp.dot(h2.astype(BF16), w1_ref[...], preferred_element_type=F32), 0.0)
    x2 = x1 + mod[5:6] * jnp.dot((u * u).astype(BF16), w2_ref[...], preferred_element_type=F32)
    inv = lax.rsqrt(jnp.mean(x2 * x2, axis=-1, keepdims=True) + RMS_EPS)
    y_ref[0] = (x2 * inv) * fg_ref[...]


def _post_attention(x, a_o, b_o, mod, mod_row, n1, n2, fg, wg, wa, wb, wo, w1, w2, tm):
    b, t, d = x.shape
    tok = lambda w: pl.BlockSpec((1, tm, w), lambda i, j: (i, j, 0))
    weights = [wg, wa, wb, wo, w1, w2]
    return pl.pallas_call(
        _post_kernel,
        grid=(b, t // tm),
        in_specs=[tok(d), tok(A_WIDTH), tok(B_WIDTH),
                  pl.BlockSpec((1, N_MOD, d), lambda i, j: (mod_row(i), 0, 0)),
                  _const_spec((1, d)), _const_spec((1, d)), _const_spec((1, d))]
                 + [_const_spec(w.shape) for w in weights],
        out_specs=tok(d),
        out_shape=jax.ShapeDtypeStruct(x.shape, F32),
        compiler_params=pltpu.CompilerParams(dimension_semantics=("arbitrary", "arbitrary"),
                                             vmem_limit_bytes=VMEM_LIMIT),
        name="post_attention",
    )(x, a_o, b_o, mod, n1, n2, fg, *weights)


def _rope_tables(t):
    pos = jnp.arange(t, dtype=jnp.int32)
    row = (pos // GRID_W).astype(F32)
    col = (pos % GRID_W).astype(F32)
    inv = ROPE_BASE ** (-jnp.arange(ROPE_PAIRS, dtype=F32) / ROPE_PAIRS)
    ar = row[:, None] * inv
    ac = col[:, None] * inv
    cos = jnp.concatenate([jnp.cos(ar), jnp.cos(ar), jnp.cos(ac), jnp.cos(ac)], axis=-1)
    sin = jnp.concatenate([-jnp.sin(ar), jnp.sin(ar), -jnp.sin(ac), jnp.sin(ac)], axis=-1)
    return jnp.tile(cos, (1, LANES // HEAD_DIM)), jnp.tile(sin, (1, LANES // HEAD_DIM))


def _dup_heads(x):
    b, t, kv, dh = x.shape
    return jnp.broadcast_to(x[:, :, :, None, :], (b, t, kv, 2, dh)).reshape(b, t, kv * 2 * dh).astype(BF16)


def kernel(x_prompt, x_sample, cache_a_k, cache_a_v, cache_b_k, cache_b_v, c, c_ctx, w_mod, b_mod,
           norm1_g, norm2_g, w_in, q_norm_g, k_norm_g, nat_bias, w_br_a, w_br_b, w_out, w_mlp_in,
           w_mlp_out, final_norm_g):
    assert w_mod.shape[0] == 1, "single-layer trunk"
    nb, seq, d = x_prompt.shape
    nd, dseq, _ = x_sample.shape
    past = cache_a_k.shape[2]

    ctx_row = nd
    pad = (-(nd + 1)) % 8
    cvec = jnp.concatenate([c, c_ctx[None, :], jnp.zeros((pad, d), F32)], axis=0)
    mod = _modulation(cvec, w_mod[0], b_mod[0]).reshape(cvec.shape[0], N_MOD, d)

    w_qkv = w_in[0, :, :QKV_WIDTH].astype(BF16)
    w_gate = w_in[0, :, QKV_WIDTH:].astype(BF16)
    wa = w_br_a[0].astype(BF16)
    wb = w_br_b[0].astype(BF16)
    wo = w_out[0].astype(BF16)
    w1 = w_mlp_in[0].astype(BF16)
    w2 = w_mlp_out[0].astype(BF16)
    n1 = norm1_g[0].reshape(1, d)
    n2 = norm2_g[0].reshape(1, d)
    fg = final_norm_g.reshape(1, d)
    qg = (jnp.tile(q_norm_g[0], A_HEADS) * Q_SCALE).reshape(1, A_WIDTH)
    kg = jnp.tile(k_norm_g[0], A_KV_HEADS).reshape(1, A_KV_WIDTH)
    head_id = np.arange(LANES) // HEAD_DIM
    hm = jnp.asarray(head_id[:, None] == head_id[None, :], dtype=BF16)

    bias = _expand_bias(nat_bias[0])

    ctx_mod_row = lambda i: ctx_row
    (qa, ka, va, qb, kb, vb, new_ak, new_av, new_bk, new_bv) = _pre_attention(
        x_prompt, mod, ctx_mod_row, n1, w_qkv, qg, kg, hm, None, True, seq)
    a_o = _shared_attention(qa, [(ka, va)], seq, [seq])
    b_o = _full_attention(qb, [(kb, vb)], PAIRS, seq, [seq])
    y_prompt = _post_attention(x_prompt, a_o, b_o, mod, ctx_mod_row, n1, n2, fg,
                               w_gate, wa, wb, wo, w1, w2, seq)

    lat_mod_row = lambda i: i
    (qa, ka, va, qb, kb, vb) = _pre_attention(
        x_sample, mod, lat_mod_row, n1, w_qkv, qg, kg, hm, _rope_tables(dseq), False, 256)
    ca_k = _dup_heads(cache_a_k[:, 0])
    ca_vt = jnp.transpose(cache_a_v[:, 0], (0, 2, 3, 1)).astype(BF16)
    cb_k = cache_b_k[:, 0].reshape(nd, past, B_WIDTH).astype(BF16)
    cb_v = cache_b_v[:, 0].reshape(nd, past, B_WIDTH).astype(BF16)
    a_o = _shared_attention(qa, [(ca_k, ca_vt), (ka, va)], 512, [256, 256])
    b_o = _nat_attention(qb, kb, vb, cb_k, cb_v, bias)
    y_sample = _post_attention(x_sample, a_o, b_o, mod, lat_mod_row, n1, n2, fg,
                               w_gate, wa, wb, wo, w1, w2, 256)

    return (y_prompt, y_sample,
            new_ak.reshape(nb, 1, seq, A_KV_HEADS, HEAD_DIM),
            new_av.reshape(nb, 1, seq, A_KV_HEADS, HEAD_DIM),
            new_bk.reshape(nb, 1, seq, B_HEADS, HEAD_DIM),
            new_bv.reshape(nb, 1, seq, B_HEADS, HEAD_DIM))
```

```python
import functools

import numpy as np
import jax
import jax.numpy as jnp
from jax import lax
from jax.experimental import pallas as pl
from jax.experimental.pallas import tpu as pltpu

F32 = jnp.float32
BF16 = jnp.bfloat16

D_MODEL = 1024
GRID_W = 64
HEAD_DIM = 64
A_HEADS = 8
A_KV_HEADS = 2
B_HEADS = 8
A_WIDTH = A_HEADS * HEAD_DIM
A_KV_WIDTH = A_KV_HEADS * HEAD_DIM
B_WIDTH = B_HEADS * HEAD_DIM
D_FF = 4 * D_MODEL
WIN_H = 8
WIN_W = 16
ROPE_BASE = 10000.0
ROPE_PAIRS = HEAD_DIM // 4
RMS_EPS = 1e-6
N_MOD = 6
NEG_INF = -1e30

LANES = 128
PAIRS = A_WIDTH // LANES
QKV_WIDTH = A_WIDTH + 2 * A_KV_WIDTH + 3 * B_WIDTH
GATE_WIDTH = 2 * D_MODEL
NAT_Q_ROWS = 4
NAT_K_ROWS = 12
VMEM_LIMIT = 56 * 1024 * 1024
LOG2E = 1.4426950408889634
Q_SCALE = HEAD_DIM ** -0.5 * LOG2E


def _const_spec(shape):
    nd = len(shape)
    return pl.BlockSpec(shape, lambda *_: (0,) * nd, pipeline_mode=pl.Buffered(1))


def _rms_mod(x, g, scale, shift):
    inv = lax.rsqrt(jnp.mean(x * x, axis=-1, keepdims=True) + RMS_EPS)
    return (x * inv) * g * (1.0 + scale) + shift


def _mod_kernel(c_ref, w_ref, b_ref, o_ref):
    cv = c_ref[...]
    s = cv * jax.nn.sigmoid(cv)
    o_ref[...] = jnp.dot(s.astype(BF16), w_ref[...].astype(BF16),
                         preferred_element_type=F32) + b_ref[...]


def _modulation(cvec, w_mod, b_mod):
    n, d = cvec.shape
    width = w_mod.shape[1]
    tn = 1024
    return pl.pallas_call(
        _mod_kernel,
        grid=(width // tn,),
        in_specs=[pl.BlockSpec((n, d), lambda j: (0, 0)),
                  pl.BlockSpec((d, tn), lambda j: (0, j)),
                  pl.BlockSpec((1, tn), lambda j: (0, j))],
        out_specs=pl.BlockSpec((n, tn), lambda j: (0, j)),
        out_shape=jax.ShapeDtypeStruct((n, width), F32),
        compiler_params=pltpu.CompilerParams(dimension_semantics=("arbitrary",),
                                             vmem_limit_bytes=VMEM_LIMIT),
        name="modulation",
    )(cvec, w_mod, b_mod.reshape(1, width))


def _nat_variant_geometry(variant):
    if variant == 0:
        return 0, (lambda i, n: n < WIN_H)
    if variant == 1:
        return -(WIN_H // 2), (lambda i, n: 0 <= n - i < WIN_H)
    return -WIN_H, (lambda i, n: n >= NAT_K_ROWS - WIN_H)


def _bias_kernel(r_ref, o_ref):
    h = pl.program_id(0)
    n_dr = 2 * WIN_H - 1
    n_dc = 2 * WIN_W - 1
    qc = lax.broadcasted_iota(jnp.int32, (GRID_W, LANES), 0)
    lane = lax.broadcasted_iota(jnp.int32, (GRID_W, LANES), 1)
    kc = lane & (GRID_W - 1)
    hi_half = lane >= GRID_W
    dc = kc - qc + (WIN_W - 1)
    cstart = jnp.clip(qc - WIN_W // 2, 0, GRID_W - WIN_W)
    col_ok = (kc >= cstart) & (kc < cstart + WIN_W)
    neg = jnp.full((GRID_W, LANES), NEG_INF, F32)
    base = h * (n_dr * n_dc)
    tiles = []
    for a in range(n_dr):
        val = neg
        for b in range(n_dc):
            val = jnp.where(dc == b, r_ref[base + a * n_dc + b] * LOG2E, val)
        tiles.append(jnp.where(col_ok, val, neg))
    for variant in range(3):
        off, valid = _nat_variant_geometry(variant)
        for i in range(NAT_Q_ROWS):
            for p in range(NAT_K_ROWS // 2):
                halves = []
                for n in (2 * p, 2 * p + 1):
                    a = n - i + off + WIN_H - 1
                    halves.append(tiles[a] if valid(i, n) else neg)
                o_ref[variant, 0, i * GRID_W:(i + 1) * GRID_W, p * LANES:(p + 1) * LANES] = (
                    jnp.where(hi_half, halves[1], halves[0]))


def _expand_bias(nat_bias):
    heads = nat_bias.shape[0]
    nq = NAT_Q_ROWS * GRID_W
    nk = NAT_K_ROWS * GRID_W
    return pl.pallas_call(
        _bias_kernel,
        grid=(heads,),
        in_specs=[pl.BlockSpec(memory_space=pltpu.SMEM)],
        out_specs=pl.BlockSpec((3, 1, nq, nk), lambda h: (0, h, 0, 0)),
        out_shape=jax.ShapeDtypeStruct((3, heads, nq, nk), F32),
        compiler_params=pltpu.CompilerParams(dimension_semantics=("arbitrary",),
                                             vmem_limit_bytes=VMEM_LIMIT),
        name="bias_expand",
    )(nat_bias.reshape(-1))


def _pre_kernel(*refs, rope, emit_cache):
    x_ref, mod_ref, n1_ref, w_ref, qg_ref, kg_ref, hm_ref = refs[:7]
    pos = 7
    if rope:
        cos_ref, sin_ref = refs[pos:pos + 2]
        pos += 2
    qa_ref, ka_ref, va_ref, qb_ref, kb_ref, vb_ref = refs[pos:pos + 6]
    pos += 6
    if emit_cache:
        cak_ref, cav_ref, cbk_ref, cbv_ref = refs[pos:pos + 4]

    x = x_ref[0]
    mod = mod_ref[0]
    h = _rms_mod(x, n1_ref[...], mod[1:2], mod[0:1])
    acc = jnp.dot(h.astype(BF16), w_ref[...], preferred_element_type=F32)
    tm = x.shape[0]
    lane = lax.broadcasted_iota(jnp.int32, (tm, LANES), 1)
    lo_half = lane < HEAD_DIM
    hm = hm_ref[...]

    def head_norm(y, g):
        ms = jnp.dot((y * y).astype(BF16), hm, preferred_element_type=F32) * (1.0 / HEAD_DIM)
        return (y * lax.rsqrt(ms + RMS_EPS)) * g

    def rotary(y):
        partner = jnp.where((lane & ROPE_PAIRS) == 0,
                            pltpu.roll(y, LANES - ROPE_PAIRS, 1), pltpu.roll(y, ROPE_PAIRS, 1))
        return y * cos_ref[...] + partner * sin_ref[...]

    def store_dup(ref, y):
        r = pltpu.roll(y, HEAD_DIM, 1)
        ref[0, :, 0:LANES] = jnp.where(lo_half, y, r).astype(ref.dtype)
        ref[0, :, LANES:2 * LANES] = jnp.where(lo_half, r, y).astype(ref.dtype)

    for c in range(PAIRS):
        sl = slice(c * LANES, (c + 1) * LANES)
        y = head_norm(acc[:, sl], qg_ref[:, sl])
        if rope:
            y = rotary(y)
        qa_ref[0, :, sl] = y.astype(qa_ref.dtype)

    k = head_norm(acc[:, A_WIDTH:A_WIDTH + A_KV_WIDTH], kg_ref[...])
    v = acc[:, A_WIDTH + A_KV_WIDTH:A_WIDTH + 2 * A_KV_WIDTH]
    if emit_cache:
        cak_ref[0] = k
        cav_ref[0] = v
    if rope:
        k = rotary(k)
    store_dup(ka_ref, k)
    vt = v.T
    for g in range(A_KV_HEADS):
        va_ref[0, g] = vt[g * HEAD_DIM:(g + 1) * HEAD_DIM].astype(va_ref.dtype)

    b0 = A_WIDTH + 2 * A_KV_WIDTH
    bq = acc[:, b0:b0 + B_WIDTH]
    bk = acc[:, b0 + B_WIDTH:b0 + 2 * B_WIDTH]
    bv = acc[:, b0 + 2 * B_WIDTH:b0 + 3 * B_WIDTH]
    qb_ref[0] = (bq * Q_SCALE).astype(qb_ref.dtype)
    kb_ref[0] = bk.astype(kb_ref.dtype)
    vb_ref[0] = bv.astype(vb_ref.dtype)
    if emit_cache:
        cbk_ref[0] = bk
        cbv_ref[0] = bv


def _pre_attention(x, mod, mod_row, n1, w_qkv, qg, kg, hm, rope_tables, emit_cache, tm):
    b, t, d = x.shape
    rope = rope_tables is not None
    tok = lambda w: pl.BlockSpec((1, tm, w), lambda i, j: (i, j, 0))
    in_specs = [tok(d),
                pl.BlockSpec((1, N_MOD, d), lambda i, j: (mod_row(i), 0, 0)),
                _const_spec((1, d)), _const_spec(w_qkv.shape), _const_spec(qg.shape),
                _const_spec(kg.shape), _const_spec(hm.shape)]
    args = [x, mod, n1, w_qkv, qg, kg, hm]
    if rope:
        in_specs += [pl.BlockSpec((tm, LANES), lambda i, j: (j, 0))] * 2
        args += list(rope_tables)
    widths = [A_WIDTH, 2 * LANES, None, B_WIDTH, B_WIDTH, B_WIDTH]
    vt_spec = pl.BlockSpec((1, A_KV_HEADS, HEAD_DIM, tm), lambda i, j: (i, 0, 0, j))
    out_specs = [vt_spec if w is None else tok(w) for w in widths]
    out_shape = [jax.ShapeDtypeStruct((b, A_KV_HEADS, HEAD_DIM, t) if w is None else (b, t, w), BF16)
                 for w in widths]
    if emit_cache:
        cw = [A_KV_WIDTH, A_KV_WIDTH, B_WIDTH, B_WIDTH]
        out_specs += [tok(w) for w in cw]
        out_shape += [jax.ShapeDtypeStruct((b, t, w), F32) for w in cw]
    return pl.pallas_call(
        functools.partial(_pre_kernel, rope=rope, emit_cache=emit_cache),
        grid=(b, t // tm),
        in_specs=in_specs, out_specs=out_specs, out_shape=out_shape,
        compiler_params=pltpu.CompilerParams(dimension_semantics=("arbitrary", "arbitrary"),
                                             vmem_limit_bytes=VMEM_LIMIT),
        name="pre_attention",
    )(*args)


def _flash_pairs(q, kv_sets, online, bias=None):
    tq, width = q.shape
    npairs = width // LANES
    lo_half = lax.broadcasted_iota(jnp.int32, (tq, LANES), 1) < HEAD_DIM
    zero = jnp.zeros((tq, LANES), q.dtype)
    q_tiles = [q[:, p * LANES:(p + 1) * LANES] for p in range(npairs)]
    stacked = [jnp.concatenate([jnp.where(lo_half, t, zero), jnp.where(lo_half, zero, t)], axis=0)
               for t in q_tiles]
    items = [(si, start, chunk, pair)
             for si, (_, _, n_keys, chunk) in enumerate(kv_sets)
             for start in range(0, n_keys, chunk)
             for pair in range(npairs)]

    def scores(item):
        si, start, size, pair = item
        k = kv_sets[si][0](pair, start, size)
        s = lax.dot_general(stacked[pair], k, (((1,), (1,)), ((), ())), preferred_element_type=F32)
        if bias is not None and si == bias[0]:
            s = s + bias[1](pair)
        return s

    carries = [None] * npairs
    pending = None
    for idx, item in enumerate(items):
        s = pending if pending is not None else scores(item)
        pending = None
        si, start, size, pair = item
        if (online or carries[pair] is None) and idx + 1 < len(items):
            pending = scores(items[idx + 1])
        v = kv_sets[si][1](pair, start, size)
        lo_k = lax.broadcasted_iota(jnp.int32, v.shape, 1) < HEAD_DIM
        one = jnp.ones_like(v)
        v_lo = jnp.where(lo_k, v, one)
        v_hi = jnp.where(lo_k, one, v)
        m_cur = jnp.max(s, axis=1, keepdims=True)
        if carries[pair] is None:
            shift = m_seen = m_cur
            p = jnp.exp2(s - shift).astype(BF16)
            acc_lo = jnp.dot(p[:tq], v_lo, preferred_element_type=F32)
            acc_hi = jnp.dot(p[tq:], v_hi, preferred_element_type=F32)
        elif online:
            m_prev, _, acc_lo, acc_hi = carries[pair]
            shift = m_seen = jnp.maximum(m_prev, m_cur)
            alpha = jnp.exp2(m_prev - shift)
            p = jnp.exp2(s - shift).astype(BF16)
            acc_lo = acc_lo * alpha[:tq] + jnp.dot(p[:tq], v_lo, preferred_element_type=F32)
            acc_hi = acc_hi * alpha[tq:] + jnp.dot(p[tq:], v_hi, preferred_element_type=F32)
        else:
            shift, m_seen, acc_lo, acc_hi = carries[pair]
            m_seen = jnp.maximum(m_seen, m_cur)
            p = jnp.exp2(s - shift).astype(BF16)
            acc_lo = acc_lo + jnp.dot(p[:tq], v_lo, preferred_element_type=F32)
            acc_hi = acc_hi + jnp.dot(p[tq:], v_hi, preferred_element_type=F32)
        carries[pair] = (shift, m_seen, acc_lo, acc_hi)

    outs = []
    for _, _, acc_lo, acc_hi in carries:
        r_lo = acc_lo / jnp.where(lo_half, pltpu.roll(acc_lo, HEAD_DIM, 1), 1.0)
        r_hi = acc_hi / jnp.where(lo_half, 1.0, pltpu.roll(acc_hi, HEAD_DIM, 1))
        outs.append(jnp.where(lo_half, r_lo, r_hi))
    excess = functools.reduce(jnp.maximum, [jnp.max(m_seen - shift) for shift, m_seen, _, _ in carries])
    return outs, excess


ONES_ROWS = 16


def _flash_shared_t(q, kv_sets, online):
    tq, width = q.shape
    npairs = width // LANES
    lo_half = lax.broadcasted_iota(jnp.int32, (tq, LANES), 1) < HEAD_DIM
    zero = jnp.zeros((tq, LANES), q.dtype)
    q_tiles = [q[:, p * LANES:(p + 1) * LANES] for p in range(npairs)]
    stacked = jnp.concatenate([jnp.where(lo_half, t, zero) for t in q_tiles]
                              + [jnp.where(lo_half, zero, t) for t in q_tiles], axis=0)
    items = [(si, start, chunk)
             for si, (_, _, n_keys, chunk) in enumerate(kv_sets)
             for start in range(0, n_keys, chunk)]

    def scores(item):
        si, start, size = item
        k = kv_sets[si][0](start, size)
        return lax.dot_general(k, stacked, (((1,), (1,)), ((), ())), preferred_element_type=F32)

    carry = None
    pending = None
    for idx, (si, start, size) in enumerate(items):
        s = pending if pending is not None else scores(items[idx])
        pending = None
        if (online or idx == 0) and idx + 1 < len(items):
            pending = scores(items[idx + 1])
        vt = kv_sets[si][1](start, size)
        v_aug = jnp.concatenate([vt, jnp.ones((ONES_ROWS, size), vt.dtype)], axis=0)
        m_cur = jnp.max(s, axis=0, keepdims=True)
        if carry is None:
            shift = m_cur
            acc = jnp.dot(v_aug, jnp.exp2(s - shift).astype(BF16), preferred_element_type=F32)
            carry = (shift, m_cur, acc)
        elif online:
            m_prev, _, acc = carry
            m_new = jnp.maximum(m_prev, m_cur)
            acc = acc * jnp.exp2(m_prev - m_new) + jnp.dot(
                v_aug, jnp.exp2(s - m_new).astype(BF16), preferred_element_type=F32)
            carry = (m_new, m_new, acc)
        else:
            shift, m_seen, acc = carry
            acc = acc + jnp.dot(v_aug, jnp.exp2(s - shift).astype(BF16), preferred_element_type=F32)
            carry = (shift, jnp.maximum(m_seen, m_cur), acc)

    shift, m_seen, acc = carry
    excess = jnp.max(m_seen - shift)
    o_t = acc[:HEAD_DIM] / acc[HEAD_DIM:HEAD_DIM + 1]
    outs = []
    for p in range(npairs):
        pair_t = jnp.concatenate([o_t[:, p * tq:(p + 1) * tq],
                                  o_t[:, (npairs + p) * tq:(npairs + p + 1) * tq]], axis=0)
        outs.append(pair_t.T)
    return outs, excess


MAX_SHIFT_EXCESS = 32.0


def _attn_shared_kernel(q_ref, *refs, chunks):
    o_ref = refs[-1]
    kv = refs[:-1]
    kv_sets = []
    for i, chunk in enumerate(chunks):
        k_ref, vt_ref = kv[2 * i], kv[2 * i + 1]
        kv_sets.append((lambda start, size, r=k_ref: r[0, start:start + size, :],
                        lambda start, size, r=vt_ref: r[0, 0, :, start:start + size],
                        k_ref.shape[1], chunk))
    _store_checked(o_ref, lambda online: _flash_shared_t(q_ref[0], kv_sets, online))


def _shared_attention(q, kv_sets, tq, chunks):
    b, t, _ = q.shape
    pairs_per_step = PAIRS // A_KV_HEADS
    qspec = pl.BlockSpec((1, tq, pairs_per_step * LANES), lambda i, c, j: (i, j, c))
    in_specs = [qspec]
    args = [q]
    for k, vt in kv_sets:
        nk = k.shape[1]
        in_specs += [pl.BlockSpec((1, nk, LANES), lambda i, c, j: (i, 0, c)),
                     pl.BlockSpec((1, 1, HEAD_DIM, nk), lambda i, c, j: (i, c, 0, 0))]
        args += [k, vt]
    return pl.pallas_call(
        functools.partial(_attn_shared_kernel, chunks=tuple(chunks)),
        grid=(b, A_KV_HEADS, t // tq),
        in_specs=in_specs, out_specs=qspec,
        out_shape=jax.ShapeDtypeStruct(q.shape, BF16),
        compiler_params=pltpu.CompilerParams(dimension_semantics=("arbitrary",) * 3,
                                             vmem_limit_bytes=VMEM_LIMIT),
        name="shared_attention",
    )(*args)


def _ref_loader(ref):
    return lambda tile, start, size: ref[0, start:start + size, tile * LANES:(tile + 1) * LANES]


def _store_pairs(o_ref, outs):
    for p, o in enumerate(outs):
        o_ref[0, :, p * LANES:(p + 1) * LANES] = o.astype(o_ref.dtype)


def _store_checked(o_ref, attend):
    outs, excess = attend(False)
    _store_pairs(o_ref, outs)

    @pl.when(jnp.logical_not(excess <= MAX_SHIFT_EXCESS))
    def _():
        _store_pairs(o_ref, attend(True)[0])


def _attn_full_kernel(q_ref, *refs, chunks):
    o_ref = refs[-1]
    kv = refs[:-1]
    kv_sets = [(_ref_loader(kv[2 * i]), _ref_loader(kv[2 * i + 1]), kv[2 * i].shape[1], chunks[i])
               for i in range(len(chunks))]
    _store_checked(o_ref, lambda online: _flash_pairs(q_ref[0], kv_sets, online))


def _full_attention(q, kv_sets, pairs_per_step, tq, chunks):
    b, t, _ = q.shape
    qspec = pl.BlockSpec((1, tq, pairs_per_step * LANES), lambda i, c, j: (i, j, c))
    in_specs = [qspec]
    args = [q]
    for k, v in kv_sets:
        spec = pl.BlockSpec((1, k.shape[1], pairs_per_step * LANES), lambda i, c, j: (i, 0, c))
        in_specs += [spec, spec]
        args += [k, v]
    return pl.pallas_call(
        functools.partial(_attn_full_kernel, chunks=tuple(chunks)),
        grid=(b, PAIRS // pairs_per_step, t // tq),
        in_specs=in_specs, out_specs=qspec,
        out_shape=jax.ShapeDtypeStruct(q.shape, BF16),
        compiler_params=pltpu.CompilerParams(dimension_semantics=("arbitrary",) * 3,
                                             vmem_limit_bytes=VMEM_LIMIT),
        name="full_attention",
    )(*args)


NAT_PAIRS_PER_STEP = 2


def _attn_nat_kernel(q_ref, kw_ref, vw_ref, kc_ref, vc_ref, bias_ref, o_ref, *, rows):
    j = pl.program_id(2)
    ws = jnp.clip(j * NAT_Q_ROWS - WIN_H // 2, 0, rows - NAT_K_ROWS)
    win = pl.multiple_of(ws * GRID_W, GRID_W)
    nk = NAT_K_ROWS * GRID_W
    tq = q_ref.shape[1]

    def window(ref):
        return lambda tile, start, size: ref[0, pl.ds(win, nk), tile * LANES:(tile + 1) * LANES]

    kv_sets = [(_ref_loader(kc_ref), _ref_loader(vc_ref), kc_ref.shape[1], kc_ref.shape[1]),
               (window(kw_ref), window(vw_ref), nk, nk)]
    load_bias = lambda gi: bias_ref[0, 2 * gi:2 * gi + 2].reshape(2 * tq, nk)
    _store_checked(o_ref, lambda online: _flash_pairs(q_ref[0], kv_sets, online, bias=(1, load_bias)))


def _nat_attention(q, k, v, ctx_k, ctx_v, bias):
    b, t, _ = q.shape
    rows = t // GRID_W
    nblk = rows // NAT_Q_ROWS
    tq = NAT_Q_ROWS * GRID_W
    nk = NAT_K_ROWS * GRID_W
    width = NAT_PAIRS_PER_STEP * LANES
    qspec = pl.BlockSpec((1, tq, width), lambda i, c, j: (i, j, c))
    whole = lambda n: pl.BlockSpec((1, n, width), lambda i, c, j: (i, 0, c))

    def bias_map(i, c, j):
        variant = jnp.where(j == 0, 0, jnp.where(j == nblk - 1, 2, 1))
        return (variant, c, 0, 0)

    return pl.pallas_call(
        functools.partial(_attn_nat_kernel, rows=rows),
        grid=(b, PAIRS // NAT_PAIRS_PER_STEP, nblk),
        in_specs=[qspec, whole(t), whole(t), whole(ctx_k.shape[1]), whole(ctx_v.shape[1]),
                  pl.BlockSpec((1, 2 * NAT_PAIRS_PER_STEP, tq, nk), bias_map)],
        out_specs=qspec,
        out_shape=jax.ShapeDtypeStruct(q.shape, BF16),
        compiler_params=pltpu.CompilerParams(dimension_semantics=("arbitrary",) * 3,
                                             vmem_limit_bytes=VMEM_LIMIT),
        name="nat_attention",
    )(q, k, v, ctx_k, ctx_v, bias)


def _post_kernel(x_ref, ao_ref, bo_ref, mod_ref, n1_ref, n2_ref, fg_ref,
                 wg_ref, wa_ref, wb_ref, wo_ref, w1_ref, w2_ref, y_ref):
    x = x_ref[0]
    mod = mod_ref[0]
    h = _rms_mod(x, n1_ref[...], mod[1:2], mod[0:1])
    gates = jnp.dot(h.astype(BF16), wg_ref[...], preferred_element_type=F32)
    ma = jnp.dot(ao_ref[0], wa_ref[...], preferred_element_type=F32)
    mb = jnp.dot(bo_ref[0], wb_ref[...], preferred_element_type=F32)
    m = jax.nn.sigmoid(gates[:, :D_MODEL]) * ma + jax.nn.sigmoid(gates[:, D_MODEL:]) * mb
    x1 = x + mod[2:3] * jnp.dot(m.astype(BF16), wo_ref[...], preferred_element_type=F32)
    h2 = _rms_mod(x1, n2_ref[...], mod[4:5], mod[3:4])
    u = jnp.maximum(jnp.dot(h2.astype(BF16), w1_ref[...], preferred_element_type=F32), 0.0)
    x2 = x1 + mod[5:6] * jnp.dot((u * u).astype(BF16), w2_ref[...], preferred_element_type=F32)
    inv = lax.rsqrt(jnp.mean(x2 * x2, axis=-1, keepdims=True) + RMS_EPS)
    y_ref[0] = (x2 * inv) * fg_ref[...]


def _post_attention(x, a_o, b_o, mod, mod_row, n1, n2, fg, wg, wa, wb, wo, w1, w2, tm):
    b, t, d = x.shape
    tok = lambda w: pl.BlockSpec((1, tm, w), lambda i, j: (i, j, 0))
    weights = [wg, wa, wb, wo, w1, w2]
    return pl.pallas_call(
        _post_kernel,
        grid=(b, t // tm),
        in_specs=[tok(d), tok(A_WIDTH), tok(B_WIDTH),
                  pl.BlockSpec((1, N_MOD, d), lambda i, j: (mod_row(i), 0, 0)),
                  _const_spec((1, d)), _const_spec((1, d)), _const_spec((1, d))]
                 + [_const_spec(w.shape) for w in weights],
        out_specs=tok(d),
        out_shape=jax.ShapeDtypeStruct(x.shape, F32),
        compiler_params=pltpu.CompilerParams(dimension_semantics=("arbitrary", "arbitrary"),
                                             vmem_limit_bytes=VMEM_LIMIT),
        name="post_attention",
    )(x, a_o, b_o, mod, n1, n2, fg, *weights)


def _rope_tables(t):
    pos = jnp.arange(t, dtype=jnp.int32)
    row = (pos // GRID_W).astype(F32)
    col = (pos % GRID_W).astype(F32)
    inv = ROPE_BASE ** (-jnp.arange(ROPE_PAIRS, dtype=F32) / ROPE_PAIRS)
    ar = row[:, None] * inv
    ac = col[:, None] * inv
    cos = jnp.concatenate([jnp.cos(ar), jnp.cos(ar), jnp.cos(ac), jnp.cos(ac)], axis=-1)
    sin = jnp.concatenate([-jnp.sin(ar), jnp.sin(ar), -jnp.sin(ac), jnp.sin(ac)], axis=-1)
    return jnp.tile(cos, (1, LANES // HEAD_DIM)), jnp.tile(sin, (1, LANES // HEAD_DIM))


def _dup_heads(x):
    b, t, kv, dh = x.shape
    return jnp.broadcast_to(x[:, :, :, None, :], (b, t, kv, 2, dh)).reshape(b, t, kv * 2 * dh).astype(BF16)


def kernel(x_prompt, x_sample, cache_a_k, cache_a_v, cache_b_k, cache_b_v, c, c_ctx, w_mod, b_mod,
           norm1_g, norm2_g, w_in, q_norm_g, k_norm_g, nat_bias, w_br_a, w_br_b, w_out, w_mlp_in,
           w_mlp_out, final_norm_g):
    assert w_mod.shape[0] == 1, "single-layer trunk"
    nb, seq, d = x_prompt.shape
    nd, dseq, _ = x_sample.shape
    past = cache_a_k.shape[2]

    ctx_row = nd
    pad = (-(nd + 1)) % 8
    cvec = jnp.concatenate([c, c_ctx[None, :], jnp.zeros((pad, d), F32)], axis=0)
    mod = _modulation(cvec, w_mod[0], b_mod[0]).reshape(cvec.shape[0], N_MOD, d)

    w_qkv = w_in[0, :, :QKV_WIDTH].astype(BF16)
    w_gate = w_in[0, :, QKV_WIDTH:].astype(BF16)
    wa = w_br_a[0].astype(BF16)
    wb = w_br_b[0].astype(BF16)
    wo = w_out[0].astype(BF16)
    w1 = w_mlp_in[0].astype(BF16)
    w2 = w_mlp_out[0].astype(BF16)
    n1 = norm1_g[0].reshape(1, d)
    n2 = norm2_g[0].reshape(1, d)
    fg = final_norm_g.reshape(1, d)
    qg = (jnp.tile(q_norm_g[0], A_HEADS) * Q_SCALE).reshape(1, A_WIDTH)
    kg = jnp.tile(k_norm_g[0], A_KV_HEADS).reshape(1, A_KV_WIDTH)
    head_id = np.arange(LANES) // HEAD_DIM
    hm = jnp.asarray(head_id[:, None] == head_id[None, :], dtype=BF16)

    bias = _expand_bias(nat_bias[0])

    ctx_mod_row = lambda i: ctx_row
    (qa, ka, va, qb, kb, vb, new_ak, new_av, new_bk, new_bv) = _pre_attention(
        x_prompt, mod, ctx_mod_row, n1, w_qkv, qg, kg, hm, None, True, seq)
    a_o = _shared_attention(qa, [(ka, va)], seq, [seq])
    b_o = _full_attention(qb, [(kb, vb)], PAIRS, seq, [seq])
    y_prompt = _post_attention(x_prompt, a_o, b_o, mod, ctx_mod_row, n1, n2, fg,
                               w_gate, wa, wb, wo, w1, w2, seq)

    lat_mod_row = lambda i: i
    (qa, ka, va, qb, kb, vb) = _pre_attention(
        x_sample, mod, lat_mod_row, n1, w_qkv, qg, kg, hm, _rope_tables(dseq), False, 256)
    ca_k = _dup_heads(cache_a_k[:, 0])
    ca_vt = jnp.transpose(cache_a_v[:, 0], (0, 2, 3, 1)).astype(BF16)
    cb_k = cache_b_k[:, 0].reshape(nd, past, B_WIDTH).astype(BF16)
    cb_v = cache_b_v[:, 0].reshape(nd, past, B_WIDTH).astype(BF16)
    a_o = _shared_attention(qa, [(ca_k, ca_vt), (ka, va)], 512, [256, 256])
    b_o = _nat_attention(qb, kb, vb, cb_k, cb_v, bias)
    y_sample = _post_attention(x_sample, a_o, b_o, mod, lat_mod_row, n1, n2, fg,
                               w_gate, wa, wb, wo, w1, w2, 256)

    return (y_prompt, y_sample,
            new_ak.reshape(nb, 1, seq, A_KV_HEADS, HEAD_DIM),
            new_av.reshape(nb, 1, seq, A_KV_HEADS, HEAD_DIM),
            new_bk.reshape(nb, 1, seq, B_HEADS, HEAD_DIM),
            new_bv.reshape(nb, 1, seq, B_HEADS, HEAD_DIM))
```

```python
import functools

import numpy as np
import jax
import jax.numpy as jnp
from jax import lax
from jax.experimental import pallas as pl
from jax.experimental.pallas import tpu as pltpu

F32 = jnp.float32
BF16 = jnp.bfloat16

D_MODEL = 1024
GRID_W = 64
HEAD_DIM = 64
A_HEADS = 8
A_KV_HEADS = 2
B_HEADS = 8
A_WIDTH = A_HEADS * HEAD_DIM
A_KV_WIDTH = A_KV_HEADS * HEAD_DIM
B_WIDTH = B_HEADS * HEAD_DIM
D_FF = 4 * D_MODEL
WIN_H = 8
WIN_W = 16
ROPE_BASE = 10000.0
ROPE_PAIRS = HEAD_DIM // 4
RMS_EPS = 1e-6
N_MOD = 6
NEG_INF = -1e30

LANES = 128
PAIRS = A_WIDTH // LANES
QKV_WIDTH = A_WIDTH + 2 * A_KV_WIDTH + 3 * B_WIDTH
GATE_WIDTH = 2 * D_MODEL
NAT_Q_ROWS = 4
NAT_K_ROWS = 12
VMEM_LIMIT = 56 * 1024 * 1024
LOG2E = 1.4426950408889634
Q_SCALE = HEAD_DIM ** -0.5 * LOG2E


def _const_spec(shape):
    nd = len(shape)
    return pl.BlockSpec(shape, lambda *_: (0,) * nd, pipeline_mode=pl.Buffered(1))


def _rms_mod(x, g, scale, shift):
    inv = lax.rsqrt(jnp.mean(x * x, axis=-1, keepdims=True) + RMS_EPS)
    return (x * inv) * g * (1.0 + scale) + shift


def _mod_kernel(c_ref, w_ref, b_ref, o_ref):
    cv = c_ref[...]
    s = cv * jax.nn.sigmoid(cv)
    o_ref[...] = jnp.dot(s.astype(BF16), w_ref[...].astype(BF16),
                         preferred_element_type=F32) + b_ref[...]


def _modulation(cvec, w_mod, b_mod):
    n, d = cvec.shape
    width = w_mod.shape[1]
    tn = 1024
    return pl.pallas_call(
        _mod_kernel,
        grid=(width // tn,),
        in_specs=[pl.BlockSpec((n, d), lambda j: (0, 0)),
                  pl.BlockSpec((d, tn), lambda j: (0, j)),
                  pl.BlockSpec((1, tn), lambda j: (0, j))],
        out_specs=pl.BlockSpec((n, tn), lambda j: (0, j)),
        out_shape=jax.ShapeDtypeStruct((n, width), F32),
        compiler_params=pltpu.CompilerParams(dimension_semantics=("arbitrary",),
                                             vmem_limit_bytes=VMEM_LIMIT),
        name="modulation",
    )(cvec, w_mod, b_mod.reshape(1, width))


def _nat_variant_geometry(variant):
    if variant == 0:
        return 0, (lambda i, n: n < WIN_H)
    if variant == 1:
        return -(WIN_H // 2), (lambda i, n: 0 <= n - i < WIN_H)
    return -WIN_H, (lambda i, n: n >= NAT_K_ROWS - WIN_H)


def _bias_kernel(r_ref, o_ref):
    h = pl.program_id(0)
    n_dr = 2 * WIN_H - 1
    n_dc = 2 * WIN_W - 1
    kc = lax.broadcasted_iota(jnp.int32, (GRID_W, LANES), 0)
    lane = lax.broadcasted_iota(jnp.int32, (GRID_W, LANES), 1)
    qc = lane & (GRID_W - 1)
    hi_half = lane >= GRID_W
    dc = kc - qc + (WIN_W - 1)
    cstart = jnp.clip(qc - WIN_W // 2, 0, GRID_W - WIN_W)
    col_ok = (kc >= cstart) & (kc < cstart + WIN_W)
    neg = jnp.full((GRID_W, LANES), NEG_INF, F32)
    base = h * (n_dr * n_dc)
    tiles = []
    for a in range(n_dr):
        val = neg
        for b in range(n_dc):
            val = jnp.where(dc == b, r_ref[base + a * n_dc + b] * LOG2E, val)
        tiles.append(jnp.where(col_ok, val, neg))
    for variant in range(3):
        off, valid = _nat_variant_geometry(variant)
        for n in range(NAT_K_ROWS):
            for p in range(NAT_Q_ROWS // 2):
                halves = []
                for i in (2 * p, 2 * p + 1):
                    a = n - i + off + WIN_H - 1
                    halves.append(tiles[a] if valid(i, n) else neg)
                o_ref[variant, 0, n * GRID_W:(n + 1) * GRID_W, p * LANES:(p + 1) * LANES] = (
                    jnp.where(hi_half, halves[1], halves[0]))


def _expand_bias(nat_bias):
    heads = nat_bias.shape[0]
    nq = NAT_Q_ROWS * GRID_W
    nk = NAT_K_ROWS * GRID_W
    return pl.pallas_call(
        _bias_kernel,
        grid=(heads,),
        in_specs=[pl.BlockSpec(memory_space=pltpu.SMEM)],
        out_specs=pl.BlockSpec((3, 1, nk, nq), lambda h: (0, h, 0, 0)),
        out_shape=jax.ShapeDtypeStruct((3, heads, nk, nq), F32),
        compiler_params=pltpu.CompilerParams(dimension_semantics=("arbitrary",),
                                             vmem_limit_bytes=VMEM_LIMIT),
        name="bias_expand",
    )(nat_bias.reshape(-1))


def _pre_kernel(*refs, rope, emit_cache):
    x_ref, mod_ref, n1_ref, w_ref, qg_ref, kg_ref, hm_ref = refs[:7]
    pos = 7
    if rope:
        cos_ref, sin_ref = refs[pos:pos + 2]
        pos += 2
    qa_ref, ka_ref, va_ref, qb_ref, kb_ref, vb_ref = refs[pos:pos + 6]
    pos += 6
    if emit_cache:
        cak_ref, cav_ref, cbk_ref, cbv_ref = refs[pos:pos + 4]

    x = x_ref[0]
    mod = mod_ref[0]
    h = _rms_mod(x, n1_ref[...], mod[1:2], mod[0:1])
    acc = jnp.dot(h.astype(BF16), w_ref[...], preferred_element_type=F32)
    tm = x.shape[0]
    lane = lax.broadcasted_iota(jnp.int32, (tm, LANES), 1)
    lo_half = lane < HEAD_DIM
    hm = hm_ref[...]

    def head_norm(y, g):
        ms = jnp.dot((y * y).astype(BF16), hm, preferred_element_type=F32) * (1.0 / HEAD_DIM)
        return (y * lax.rsqrt(ms + RMS_EPS)) * g

    def rotary(y):
        partner = jnp.where((lane & ROPE_PAIRS) == 0,
                            pltpu.roll(y, LANES - ROPE_PAIRS, 1), pltpu.roll(y, ROPE_PAIRS, 1))
        return y * cos_ref[...] + partner * sin_ref[...]

    def store_dup(ref, y):
        r = pltpu.roll(y, HEAD_DIM, 1)
        ref[0, :, 0:LANES] = jnp.where(lo_half, y, r).astype(ref.dtype)
        ref[0, :, LANES:2 * LANES] = jnp.where(lo_half, r, y).astype(ref.dtype)

    for c in range(PAIRS):
        sl = slice(c * LANES, (c + 1) * LANES)
        y = head_norm(acc[:, sl], qg_ref[:, sl])
        if rope:
            y = rotary(y)
        qa_ref[0, c] = y.T.astype(qa_ref.dtype)

    k = head_norm(acc[:, A_WIDTH:A_WIDTH + A_KV_WIDTH], kg_ref[...])
    v = acc[:, A_WIDTH + A_KV_WIDTH:A_WIDTH + 2 * A_KV_WIDTH]
    if emit_cache:
        cak_ref[0] = k
        cav_ref[0] = v
    if rope:
        k = rotary(k)
    store_dup(ka_ref, k)
    vt = v.T
    for g in range(A_KV_HEADS):
        va_ref[0, g] = vt[g * HEAD_DIM:(g + 1) * HEAD_DIM].astype(va_ref.dtype)

    b0 = A_WIDTH + 2 * A_KV_WIDTH
    bq = acc[:, b0:b0 + B_WIDTH]
    bk = acc[:, b0 + B_WIDTH:b0 + 2 * B_WIDTH]
    bv = acc[:, b0 + 2 * B_WIDTH:b0 + 3 * B_WIDTH]
    for c in range(PAIRS):
        qb_ref[0, c] = (bq[:, c * LANES:(c + 1) * LANES] * Q_SCALE).T.astype(qb_ref.dtype)
    kb_ref[0] = bk.astype(kb_ref.dtype)
    vb_ref[0] = bv.astype(vb_ref.dtype)
    if emit_cache:
        cbk_ref[0] = bk
        cbv_ref[0] = bv


def _pre_attention(x, mod, mod_row, n1, w_qkv, qg, kg, hm, rope_tables, emit_cache, tm):
    b, t, d = x.shape
    rope = rope_tables is not None
    tok = lambda w: pl.BlockSpec((1, tm, w), lambda i, j: (i, j, 0))
    in_specs = [tok(d),
                pl.BlockSpec((1, N_MOD, d), lambda i, j: (mod_row(i), 0, 0)),
                _const_spec((1, d)), _const_spec(w_qkv.shape), _const_spec(qg.shape),
                _const_spec(kg.shape), _const_spec(hm.shape)]
    args = [x, mod, n1, w_qkv, qg, kg, hm]
    if rope:
        in_specs += [pl.BlockSpec((tm, LANES), lambda i, j: (j, 0))] * 2
        args += list(rope_tables)
    qt = (PAIRS, LANES)
    vt = (A_KV_HEADS, HEAD_DIM)
    layouts = [qt, 2 * LANES, vt, qt, B_WIDTH, B_WIDTH]
    out_specs = [tok(w) if isinstance(w, int) else
                 pl.BlockSpec((1,) + w + (tm,), lambda i, j: (i, 0, 0, j)) for w in layouts]
    out_shape = [jax.ShapeDtypeStruct((b, t, w) if isinstance(w, int) else (b,) + w + (t,), BF16)
                 for w in layouts]
    if emit_cache:
        cw = [A_KV_WIDTH, A_KV_WIDTH, B_WIDTH, B_WIDTH]
        out_specs += [tok(w) for w in cw]
        out_shape += [jax.ShapeDtypeStruct((b, t, w), F32) for w in cw]
    return pl.pallas_call(
        functools.partial(_pre_kernel, rope=rope, emit_cache=emit_cache),
        grid=(b, t // tm),
        in_specs=in_specs, out_specs=out_specs, out_shape=out_shape,
        compiler_params=pltpu.CompilerParams(dimension_semantics=("arbitrary", "arbitrary"),
                                             vmem_limit_bytes=VMEM_LIMIT),
        name="pre_attention",
    )(*args)


ONES_ROWS = 16

MAX_SHIFT_EXCESS = 32.0


def _flash_t(streams, kv_sets, online, bias=None):
    items = [(si, start, chunk, st)
             for si, (n_keys, chunk, _, _) in enumerate(kv_sets)
             for start in range(0, n_keys, chunk)
             for st in range(len(streams))]

    def scores(item):
        si, start, size, st = item
        k = kv_sets[si][2](st, start, size)
        s = jnp.dot(k, streams[st][0], preferred_element_type=F32)
        if bias is not None and si == bias[0]:
            s = s + bias[1](st)
        return s

    carries = [None] * len(streams)
    pending = None
    for idx, item in enumerate(items):
        s = pending if pending is not None else scores(item)
        pending = None
        si, start, size, st = item
        if (online or carries[st] is None) and idx + 1 < len(items):
            pending = scores(items[idx + 1])
        ones = jnp.ones((ONES_ROWS, size), BF16)
        v_augs = [jnp.concatenate([vt, ones], axis=0) for vt in kv_sets[si][3](st, start, size)]
        m_cur = jnp.max(s, axis=0, keepdims=True)
        alpha = None
        if carries[st] is None:
            shift, m_seen, prev = m_cur, m_cur, None
        elif online:
            m_prev, _, prev = carries[st]
            shift = m_seen = jnp.maximum(m_prev, m_cur)
            alpha = jnp.exp2(m_prev - shift)
        else:
            shift, m_seen, prev = carries[st]
            m_seen = jnp.maximum(m_seen, m_cur)
        p = jnp.exp2(s - shift).astype(BF16)
        accs = []
        for gi, (c0, cn) in enumerate(streams[st][1]):
            acc = jnp.dot(v_augs[gi], p[:, c0:c0 + cn], preferred_element_type=F32)
            if prev is not None:
                acc = (prev[gi] if alpha is None else prev[gi] * alpha[:, c0:c0 + cn]) + acc
            accs.append(acc)
        carries[st] = (shift, m_seen, accs)

    outs = [[acc[:HEAD_DIM] / acc[HEAD_DIM:HEAD_DIM + 1] for acc in accs] for _, _, accs in carries]
    excess = functools.reduce(jnp.maximum, [jnp.max(m_seen - shift) for shift, m_seen, _ in carries])
    return outs, excess


def _masked_heads(qt):
    lo_rows = lax.broadcasted_iota(jnp.int32, qt.shape, 0) < HEAD_DIM
    zero = jnp.zeros_like(qt)
    return jnp.where(lo_rows, qt, zero), jnp.where(lo_rows, zero, qt)


def _split_vt(v):
    vt = v.T
    return [vt[:HEAD_DIM], vt[HEAD_DIM:]]


def _store_checked(o_ref, attend):
    def store(outs):
        for p, o in enumerate(outs):
            o_ref[0, :, p * LANES:(p + 1) * LANES] = o.astype(o_ref.dtype)

    outs, excess = attend(False)
    store(outs)

    @pl.when(jnp.logical_not(excess <= MAX_SHIFT_EXCESS))
    def _():
        store(attend(True)[0])


def _attn_shared_kernel(q_ref, *refs, chunks):
    o_ref = refs[-1]
    kv = refs[:-1]
    npairs, tq = q_ref.shape[1], q_ref.shape[3]
    kv_sets = [(kv[2 * i].shape[1], chunk,
                lambda st, start, size, r=kv[2 * i]: r[0, start:start + size, :],
                lambda st, start, size, r=kv[2 * i + 1]: [r[0, 0, :, start:start + size]])
               for i, chunk in enumerate(chunks)]

    def attend(online):
        masked = [_masked_heads(q_ref[0, p]) for p in range(npairs)]
        q_t = jnp.concatenate([m[0] for m in masked] + [m[1] for m in masked], axis=1)
        outs, excess = _flash_t([(q_t, [(0, 2 * npairs * tq)])], kv_sets, online)
        o_t = outs[0][0]
        return [jnp.concatenate([o_t[:, p * tq:(p + 1) * tq],
                                 o_t[:, (npairs + p) * tq:(npairs + p + 1) * tq]], axis=0).T
                for p in range(npairs)], excess

    _store_checked(o_ref, attend)


def _shared_attention(qt, kv_sets, tq, chunks):
    b, _, _, t = qt.shape
    pairs_per_step = PAIRS // A_KV_HEADS
    in_specs = [pl.BlockSpec((1, pairs_per_step, LANES, tq), lambda i, c, j: (i, c, 0, j))]
    args = [qt]
    for k, vt in kv_sets:
        nk = k.shape[1]
        in_specs += [pl.BlockSpec((1, nk, LANES), lambda i, c, j: (i, 0, c)),
                     pl.BlockSpec((1, 1, HEAD_DIM, nk), lambda i, c, j: (i, c, 0, 0))]
        args += [k, vt]
    return pl.pallas_call(
        functools.partial(_attn_shared_kernel, chunks=tuple(chunks)),
        grid=(b, A_KV_HEADS, t // tq),
        in_specs=in_specs,
        out_specs=pl.BlockSpec((1, tq, pairs_per_step * LANES), lambda i, c, j: (i, j, c)),
        out_shape=jax.ShapeDtypeStruct((b, t, A_WIDTH), BF16),
        compiler_params=pltpu.CompilerParams(dimension_semantics=("arbitrary",) * 3,
                                             vmem_limit_bytes=VMEM_LIMIT),
        name="shared_attention",
    )(*args)


def _pair_streams(q_ref):
    tq = q_ref.shape[3]
    return [(jnp.concatenate(_masked_heads(q_ref[0, p]), axis=1), [(0, tq), (tq, tq)])
            for p in range(q_ref.shape[1])]


def _pair_tiles(outs):
    return [jnp.concatenate(groups, axis=0).T for groups in outs]


def _pair_loaders(k_ref, v_ref, rows):
    lanes = lambda st: slice(st * LANES, (st + 1) * LANES)
    return (lambda st, start, size: k_ref[0, rows(start, size), lanes(st)],
            lambda st, start, size: _split_vt(v_ref[0, rows(start, size), lanes(st)]))


def _attn_full_kernel(q_ref, k_ref, v_ref, o_ref):
    nk = k_ref.shape[1]
    kv_sets = [(nk, nk) + _pair_loaders(k_ref, v_ref, lambda start, size: slice(start, start + size))]

    def attend(online):
        outs, excess = _flash_t(_pair_streams(q_ref), kv_sets, online)
        return _pair_tiles(outs), excess

    _store_checked(o_ref, attend)


def _full_attention(qt, k, v):
    b, npairs, _, t = qt.shape
    tok = pl.BlockSpec((1, t, npairs * LANES), lambda i: (i, 0, 0))
    return pl.pallas_call(
        _attn_full_kernel,
        grid=(b,),
        in_specs=[pl.BlockSpec((1, npairs, LANES, t), lambda i: (i, 0, 0, 0)), tok, tok],
        out_specs=tok,
        out_shape=jax.ShapeDtypeStruct((b, t, npairs * LANES), BF16),
        compiler_params=pltpu.CompilerParams(dimension_semantics=("arbitrary",),
                                             vmem_limit_bytes=VMEM_LIMIT),
        name="full_attention",
    )(qt, k, v)


NAT_PAIRS_PER_STEP = 2


def _attn_nat_kernel(q_ref, kw_ref, vw_ref, kc_ref, vc_ref, bias_ref, o_ref, *, rows):
    j = pl.program_id(2)
    ws = jnp.clip(j * NAT_Q_ROWS - WIN_H // 2, 0, rows - NAT_K_ROWS)
    win = pl.multiple_of(ws * GRID_W, GRID_W)
    nk = NAT_K_ROWS * GRID_W
    nc = kc_ref.shape[1]
    kv_sets = [(nc, nc) + _pair_loaders(kc_ref, vc_ref, lambda start, size: slice(start, start + size)),
               (nk, nk) + _pair_loaders(kw_ref, vw_ref, lambda start, size: pl.ds(win, nk))]
    load_bias = lambda st: jnp.concatenate([bias_ref[0, 2 * st], bias_ref[0, 2 * st + 1]], axis=1)

    def attend(online):
        outs, excess = _flash_t(_pair_streams(q_ref), kv_sets, online, bias=(1, load_bias))
        return _pair_tiles(outs), excess

    _store_checked(o_ref, attend)


def _nat_attention(qt, k, v, ctx_k, ctx_v, bias):
    b, _, _, t = qt.shape
    rows = t // GRID_W
    nblk = rows // NAT_Q_ROWS
    tq = NAT_Q_ROWS * GRID_W
    nk = NAT_K_ROWS * GRID_W
    width = NAT_PAIRS_PER_STEP * LANES
    whole = lambda n: pl.BlockSpec((1, n, width), lambda i, c, j: (i, 0, c))

    def bias_map(i, c, j):
        variant = jnp.where(j == 0, 0, jnp.where(j == nblk - 1, 2, 1))
        return (variant, c, 0, 0)

    return pl.pallas_call(
        functools.partial(_attn_nat_kernel, rows=rows),
        grid=(b, PAIRS // NAT_PAIRS_PER_STEP, nblk),
        in_specs=[pl.BlockSpec((1, NAT_PAIRS_PER_STEP, LANES, tq), lambda i, c, j: (i, c, 0, j)),
                  whole(t), whole(t), whole(ctx_k.shape[1]), whole(ctx_v.shape[1]),
                  pl.BlockSpec((1, 2 * NAT_PAIRS_PER_STEP, nk, tq), bias_map)],
        out_specs=pl.BlockSpec((1, tq, width), lambda i, c, j: (i, j, c)),
        out_shape=jax.ShapeDtypeStruct((b, t, B_WIDTH), BF16),
        compiler_params=pltpu.CompilerParams(dimension_semantics=("arbitrary",) * 3,
                                             vmem_limit_bytes=VMEM_LIMIT),
        name="nat_attention",
    )(qt, k, v, ctx_k, ctx_v, bias)


def _post_kernel(x_ref, ao_ref, bo_ref, mod_ref, n1_ref, n2_ref, fg_ref,
                 wg_ref, wa_ref, wb_ref, wo_ref, w1_ref, w2_ref, y_ref):
    x = x_ref[0]
    mod = mod_ref[0]
    h = _rms_mod(x, n1_ref[...], mod[1:2], mod[0:1])
    gates = jnp.dot(h.astype(BF16), wg_ref[...], preferred_element_type=F32)
    ma = jnp.dot(ao_ref[0], wa_ref[...], preferred_element_type=F32)
    mb = jnp.dot(bo_ref[0], wb_ref[...], preferred_element_type=F32)
    m = jax.nn.sigmoid(gates[:, :D_MODEL]) * ma + jax.nn.sigmoid(gates[:, D_MODEL:]) * mb
    x1 = x + mod[2:3] * jnp.dot(m.astype(BF16), wo_ref[...], preferred_element_type=F32)
    h2 = _rms_mod(x1, n2_ref[...], mod[4:5], mod[3:4])
    u = jnp.maximum(jnp.dot(h2.astype(BF16), w1_ref[...], preferred_element_type=F32), 0.0)
    x2 = x1 + mod[5:6] * jnp.dot((u * u).astype(BF16), w2_ref[...], preferred_element_type=F32)
    inv = lax.rsqrt(jnp.mean(x2 * x2, axis=-1, keepdims=True) + RMS_EPS)
    y_ref[0] = (x2 * inv) * fg_ref[...]


def _post_attention(x, a_o, b_o, mod, mod_row, n1, n2, fg, wg, wa, wb, wo, w1, w2, tm):
    b, t, d = x.shape
    tok = lambda w: pl.BlockSpec((1, tm, w), lambda i, j: (i, j, 0))
    weights = [wg, wa, wb, wo, w1, w2]
    return pl.pallas_call(
        _post_kernel,
        grid=(b, t // tm),
        in_specs=[tok(d), tok(A_WIDTH), tok(B_WIDTH),
                  pl.BlockSpec((1, N_MOD, d), lambda i, j: (mod_row(i), 0, 0)),
                  _const_spec((1, d)), _const_spec((1, d)), _const_spec((1, d))]
                 + [_const_spec(w.shape) for w in weights],
        out_specs=tok(d),
        out_shape=jax.ShapeDtypeStruct(x.shape, F32),
        compiler_params=pltpu.CompilerParams(dimension_semantics=("arbitrary", "arbitrary"),
                                             vmem_limit_bytes=VMEM_LIMIT),
        name="post_attention",
    )(x, a_o, b_o, mod, n1, n2, fg, *weights)


def _rope_tables(t):
    pos = jnp.arange(t, dtype=jnp.int32)
    row = (pos // GRID_W).astype(F32)
    col = (pos % GRID_W).astype(F32)
    inv = ROPE_BASE ** (-jnp.arange(ROPE_PAIRS, dtype=F32) / ROPE_PAIRS)
    ar = row[:, None] * inv
    ac = col[:, None] * inv
    cos = jnp.concatenate([jnp.cos(ar), jnp.cos(ar), jnp.cos(ac), jnp.cos(ac)], axis=-1)
    sin = jnp.concatenate([-jnp.sin(ar), jnp.sin(ar), -jnp.sin(ac), jnp.sin(ac)], axis=-1)
    return jnp.tile(cos, (1, LANES // HEAD_DIM)), jnp.tile(sin, (1, LANES // HEAD_DIM))


def _dup_heads(x):
    b, t, kv, dh = x.shape
    return jnp.broadcast_to(x[:, :, :, None, :], (b, t, kv, 2, dh)).reshape(b, t, kv * 2 * dh).astype(BF16)


def kernel(x_prompt, x_sample, cache_a_k, cache_a_v, cache_b_k, cache_b_v, c, c_ctx, w_mod, b_mod,
           norm1_g, norm2_g, w_in, q_norm_g, k_norm_g, nat_bias, w_br_a, w_br_b, w_out, w_mlp_in,
           w_mlp_out, final_norm_g):
    assert w_mod.shape[0] == 1, "single-layer trunk"
    nb, seq, d = x_prompt.shape
    nd, dseq, _ = x_sample.shape
    past = cache_a_k.shape[2]

    ctx_row = nd
    pad = (-(nd + 1)) % 8
    cvec = jnp.concatenate([c, c_ctx[None, :], jnp.zeros((pad, d), F32)], axis=0)
    mod = _modulation(cvec, w_mod[0], b_mod[0]).reshape(cvec.shape[0], N_MOD, d)

    w_qkv = w_in[0, :, :QKV_WIDTH].astype(BF16)
    w_gate = w_in[0, :, QKV_WIDTH:].astype(BF16)
    wa = w_br_a[0].astype(BF16)
    wb = w_br_b[0].astype(BF16)
    wo = w_out[0].astype(BF16)
    w1 = w_mlp_in[0].astype(BF16)
    w2 = w_mlp_out[0].astype(BF16)
    n1 = norm1_g[0].reshape(1, d)
    n2 = norm2_g[0].reshape(1, d)
    fg = final_norm_g.reshape(1, d)
    qg = (jnp.tile(q_norm_g[0], A_HEADS) * Q_SCALE).reshape(1, A_WIDTH)
    kg = jnp.tile(k_norm_g[0], A_KV_HEADS).reshape(1, A_KV_WIDTH)
    head_id = np.arange(LANES) // HEAD_DIM
    hm = jnp.asarray(head_id[:, None] == head_id[None, :], dtype=BF16)

    bias = _expand_bias(nat_bias[0])

    ctx_mod_row = lambda i: ctx_row
    (qa, ka, va, qb, kb, vb, new_ak, new_av, new_bk, new_bv) = _pre_attention(
        x_prompt, mod, ctx_mod_row, n1, w_qkv, qg, kg, hm, None, True, seq)
    a_o = _shared_attention(qa, [(ka, va)], seq, [seq])
    b_o = _full_attention(qb, kb, vb)
    y_prompt = _post_attention(x_prompt, a_o, b_o, mod, ctx_mod_row, n1, n2, fg,
                               w_gate, wa, wb, wo, w1, w2, seq)

    lat_mod_row = lambda i: i
    (qa, ka, va, qb, kb, vb) = _pre_attention(
        x_sample, mod, lat_mod_row, n1, w_qkv, qg, kg, hm, _rope_tables(dseq), False, 256)
    ca_k = _dup_heads(cache_a_k[:, 0])
    ca_vt = jnp.transpose(cache_a_v[:, 0], (0, 2, 3, 1)).astype(BF16)
    cb_k = cache_b_k[:, 0].reshape(nd, past, B_WIDTH).astype(BF16)
    cb_v = cache_b_v[:, 0].reshape(nd, past, B_WIDTH).astype(BF16)
    a_o = _shared_attention(qa, [(ca_k, ca_vt), (ka, va)], 512, [256, 256])
    b_o = _nat_attention(qb, kb, vb, cb_k, cb_v, bias)
    y_sample = _post_attention(x_sample, a_o, b_o, mod, lat_mod_row, n1, n2, fg,
                               w_gate, wa, wb, wo, w1, w2, 256)

    return (y_prompt, y_sample,
            new_ak.reshape(nb, 1, seq, A_KV_HEADS, HEAD_DIM),
            new_av.reshape(nb, 1, seq, A_KV_HEADS, HEAD_DIM),
            new_bk.reshape(nb, 1, seq, B_HEADS, HEAD_DIM),
            new_bv.reshape(nb, 1, seq, B_HEADS, HEAD_DIM))
```

```python
import functools

import numpy as np
import jax
import jax.numpy as jnp
from jax import lax
from jax.experimental import pallas as pl
from jax.experimental.pallas import tpu as pltpu

F32 = jnp.float32
BF16 = jnp.bfloat16

D_MODEL = 1024
GRID_W = 64
HEAD_DIM = 64
A_HEADS = 8
A_KV_HEADS = 2
B_HEADS = 8
A_WIDTH = A_HEADS * HEAD_DIM
A_KV_WIDTH = A_KV_HEADS * HEAD_DIM
B_WIDTH = B_HEADS * HEAD_DIM
D_FF = 4 * D_MODEL
WIN_H = 8
WIN_W = 16
ROPE_BASE = 10000.0
ROPE_PAIRS = HEAD_DIM // 4
RMS_EPS = 1e-6
N_MOD = 6
NEG_INF = -1e30

LANES = 128
PAIRS = A_WIDTH // LANES
QKV_WIDTH = A_WIDTH + 2 * A_KV_WIDTH + 3 * B_WIDTH
GATE_WIDTH = 2 * D_MODEL
NAT_Q_ROWS = 4
NAT_K_ROWS = 12
PRE_SUB_ROWS = 256
VMEM_LIMIT = 56 * 1024 * 1024
LOG2E = 1.4426950408889634
Q_SCALE = HEAD_DIM ** -0.5 * LOG2E


def _const_spec(shape):
    nd = len(shape)
    return pl.BlockSpec(shape, lambda *_: (0,) * nd, pipeline_mode=pl.Buffered(1))


def _rms_mod(x, g, scale, shift):
    inv = lax.rsqrt(jnp.mean(x * x, axis=-1, keepdims=True) + RMS_EPS)
    return (x * inv) * g * (1.0 + scale) + shift


def _mod_kernel(c_ref, w_ref, b_ref, o_ref):
    cv = c_ref[...]
    s = cv * jax.nn.sigmoid(cv)
    o_ref[...] = jnp.dot(s.astype(BF16), w_ref[...].astype(BF16),
                         preferred_element_type=F32) + b_ref[...]


def _modulation(cvec, w_mod, b_mod):
    n, d = cvec.shape
    width = w_mod.shape[1]
    tn = 1024
    return pl.pallas_call(
        _mod_kernel,
        grid=(width // tn,),
        in_specs=[pl.BlockSpec((n, d), lambda j: (0, 0)),
                  pl.BlockSpec((d, tn), lambda j: (0, j)),
                  pl.BlockSpec((1, tn), lambda j: (0, j))],
        out_specs=pl.BlockSpec((n, tn), lambda j: (0, j)),
        out_shape=jax.ShapeDtypeStruct((n, width), F32),
        compiler_params=pltpu.CompilerParams(dimension_semantics=("arbitrary",),
                                             vmem_limit_bytes=VMEM_LIMIT),
        name="modulation",
    )(cvec, w_mod, b_mod.reshape(1, width))


def _nat_variant_geometry(variant):
    if variant == 0:
        return 0, (lambda i, n: n < WIN_H)
    if variant == 1:
        return -(WIN_H // 2), (lambda i, n: 0 <= n - i < WIN_H)
    return -WIN_H, (lambda i, n: n >= NAT_K_ROWS - WIN_H)


def _bias_kernel(r_ref, o_ref):
    h = pl.program_id(0)
    n_dr = 2 * WIN_H - 1
    n_dc = 2 * WIN_W - 1
    kc = lax.broadcasted_iota(jnp.int32, (GRID_W, LANES), 0)
    lane = lax.broadcasted_iota(jnp.int32, (GRID_W, LANES), 1)
    qc = lane & (GRID_W - 1)
    hi_half = lane >= GRID_W
    dc = kc - qc + (WIN_W - 1)
    cstart = jnp.clip(qc - WIN_W // 2, 0, GRID_W - WIN_W)
    col_ok = (kc >= cstart) & (kc < cstart + WIN_W)
    neg = jnp.full((GRID_W, LANES), NEG_INF, F32)
    base = h * (n_dr * n_dc)
    tiles = []
    for a in range(n_dr):
        val = neg
        for b in range(n_dc):
            val = jnp.where(dc == b, r_ref[base + a * n_dc + b] * LOG2E, val)
        tiles.append(jnp.where(col_ok, val, neg))
    for variant in range(3):
        off, valid = _nat_variant_geometry(variant)
        for n in range(NAT_K_ROWS):
            for p in range(NAT_Q_ROWS // 2):
                halves = []
                for i in (2 * p, 2 * p + 1):
                    a = n - i + off + WIN_H - 1
                    halves.append(tiles[a] if valid(i, n) else neg)
                o_ref[variant, 0, n * GRID_W:(n + 1) * GRID_W, p * LANES:(p + 1) * LANES] = (
                    jnp.where(hi_half, halves[1], halves[0]))


def _expand_bias(nat_bias):
    heads = nat_bias.shape[0]
    nq = NAT_Q_ROWS * GRID_W
    nk = NAT_K_ROWS * GRID_W
    return pl.pallas_call(
        _bias_kernel,
        grid=(heads,),
        in_specs=[pl.BlockSpec(memory_space=pltpu.SMEM)],
        out_specs=pl.BlockSpec((3, 1, nk, nq), lambda h: (0, h, 0, 0)),
        out_shape=jax.ShapeDtypeStruct((3, heads, nk, nq), F32),
        compiler_params=pltpu.CompilerParams(dimension_semantics=("arbitrary",),
                                             vmem_limit_bytes=VMEM_LIMIT),
        name="bias_expand",
    )(nat_bias.reshape(-1))


def _pre_kernel(*refs, rope, emit_cache):
    x_ref, mod_ref, n1_ref, w_ref, qg_ref, kg_ref, hm_ref = refs[:7]
    pos = 7
    if rope:
        cos_ref, sin_ref = refs[pos:pos + 2]
        pos += 2
    qa_ref, ka_ref, va_ref, qb_ref, kb_ref, vb_ref = refs[pos:pos + 6]
    pos += 6
    if emit_cache:
        cak_ref, cav_ref, cbk_ref, cbv_ref = refs[pos:pos + 4]

    mod = mod_ref[0]
    tm = x_ref.shape[1]
    sub = min(tm, PRE_SUB_ROWS)
    row_slices = [slice(r, r + sub) for r in range(0, tm, sub)]
    hs = [_rms_mod(x_ref[0, rows], n1_ref[...], mod[1:2], mod[0:1]).astype(BF16) for rows in row_slices]
    accs = [jnp.dot(h, w_ref[...], preferred_element_type=F32) for h in hs]
    lane = lax.broadcasted_iota(jnp.int32, (sub, LANES), 1)
    lo_half = lane < HEAD_DIM
    hm = hm_ref[...]

    def head_norm(y, g):
        ms = jnp.dot((y * y).astype(BF16), hm, preferred_element_type=F32) * (1.0 / HEAD_DIM)
        return (y * lax.rsqrt(ms + RMS_EPS)) * g

    for rows, acc in zip(row_slices, accs):
        def rotary(y):
            partner = jnp.where((lane & ROPE_PAIRS) == 0,
                                pltpu.roll(y, LANES - ROPE_PAIRS, 1), pltpu.roll(y, ROPE_PAIRS, 1))
            return y * cos_ref[rows] + partner * sin_ref[rows]

        def store_dup(ref, y):
            r = pltpu.roll(y, HEAD_DIM, 1)
            ref[0, rows, 0:LANES] = jnp.where(lo_half, y, r).astype(ref.dtype)
            ref[0, rows, LANES:2 * LANES] = jnp.where(lo_half, r, y).astype(ref.dtype)

        for c in range(PAIRS):
            sl = slice(c * LANES, (c + 1) * LANES)
            y = head_norm(acc[:, sl], qg_ref[:, sl])
            if rope:
                y = rotary(y)
            qa_ref[0, c, :, rows] = y.T.astype(qa_ref.dtype)

        k = head_norm(acc[:, A_WIDTH:A_WIDTH + A_KV_WIDTH], kg_ref[...])
        v = acc[:, A_WIDTH + A_KV_WIDTH:A_WIDTH + 2 * A_KV_WIDTH]
        if emit_cache:
            cak_ref[0, rows] = k
            cav_ref[0, rows] = v
        if rope:
            k = rotary(k)
        store_dup(ka_ref, k)
        vt = v.T
        for g in range(A_KV_HEADS):
            va_ref[0, g, :, rows] = vt[g * HEAD_DIM:(g + 1) * HEAD_DIM].astype(va_ref.dtype)

        b0 = A_WIDTH + 2 * A_KV_WIDTH
        bq = acc[:, b0:b0 + B_WIDTH]
        bk = acc[:, b0 + B_WIDTH:b0 + 2 * B_WIDTH]
        bv = acc[:, b0 + 2 * B_WIDTH:b0 + 3 * B_WIDTH]
        for c in range(PAIRS):
            qb_ref[0, c, :, rows] = (bq[:, c * LANES:(c + 1) * LANES] * Q_SCALE).T.astype(qb_ref.dtype)
        kb_ref[0, rows] = bk.astype(kb_ref.dtype)
        vb_ref[0, rows] = bv.astype(vb_ref.dtype)
        if emit_cache:
            cbk_ref[0, rows] = bk
            cbv_ref[0, rows] = bv


def _pre_attention(x, mod, mod_row, n1, w_qkv, qg, kg, hm, rope_tables, emit_cache, tm):
    b, t, d = x.shape
    rope = rope_tables is not None
    tok = lambda w: pl.BlockSpec((1, tm, w), lambda i, j: (i, j, 0))
    in_specs = [tok(d),
                pl.BlockSpec((1, N_MOD, d), lambda i, j: (mod_row(i), 0, 0)),
                _const_spec((1, d)), _const_spec(w_qkv.shape), _const_spec(qg.shape),
                _const_spec(kg.shape), _const_spec(hm.shape)]
    args = [x, mod, n1, w_qkv, qg, kg, hm]
    if rope:
        in_specs += [pl.BlockSpec((tm, LANES), lambda i, j: (j, 0))] * 2
        args += list(rope_tables)
    qt = (PAIRS, LANES)
    vt = (A_KV_HEADS, HEAD_DIM)
    layouts = [qt, 2 * LANES, vt, qt, B_WIDTH, B_WIDTH]
    out_specs = [tok(w) if isinstance(w, int) else
                 pl.BlockSpec((1,) + w + (tm,), lambda i, j: (i, 0, 0, j)) for w in layouts]
    out_shape = [jax.ShapeDtypeStruct((b, t, w) if isinstance(w, int) else (b,) + w + (t,), BF16)
                 for w in layouts]
    if emit_cache:
        cw = [A_KV_WIDTH, A_KV_WIDTH, B_WIDTH, B_WIDTH]
        out_specs += [tok(w) for w in cw]
        out_shape += [jax.ShapeDtypeStruct((b, t, w), F32) for w in cw]
    return pl.pallas_call(
        functools.partial(_pre_kernel, rope=rope, emit_cache=emit_cache),
        grid=(b, t // tm),
        in_specs=in_specs, out_specs=out_specs, out_shape=out_shape,
        compiler_params=pltpu.CompilerParams(dimension_semantics=("arbitrary", "arbitrary"),
                                             vmem_limit_bytes=VMEM_LIMIT),
        name="pre_attention",
    )(*args)


ONES_ROWS = 16

MAX_SHIFT_EXCESS = 32.0


def _flash_t(streams, kv_sets, online, bias=None):
    items = [(si, start, chunk, st)
             for si, (n_keys, chunk, _, _) in enumerate(kv_sets)
             for start in range(0, n_keys, chunk)
             for st in range(len(streams))]

    def scores(item):
        si, start, size, st = item
        k = kv_sets[si][2](st, start, size)
        s = jnp.dot(k, streams[st][0], preferred_element_type=F32)
        if bias is not None and si == bias[0]:
            s = s + bias[1](st)
        return s

    carries = [None] * len(streams)
    pending = None
    for idx, item in enumerate(items):
        s = pending if pending is not None else scores(item)
        pending = None
        si, start, size, st = item
        if (online or carries[st] is None) and idx + 1 < len(items):
            pending = scores(items[idx + 1])
        ones = jnp.ones((ONES_ROWS, size), BF16)
        v_augs = [jnp.concatenate([vt, ones], axis=0) for vt in kv_sets[si][3](st, start, size)]
        m_cur = jnp.max(s, axis=0, keepdims=True)
        alpha = None
        if carries[st] is None:
            shift, m_seen, prev = m_cur, m_cur, None
        elif online:
            m_prev, _, prev = carries[st]
            shift = m_seen = jnp.maximum(m_prev, m_cur)
            alpha = jnp.exp2(m_prev - shift)
        else:
            shift, m_seen, prev = carries[st]
            m_seen = jnp.maximum(m_seen, m_cur)
        p = jnp.exp2(s - shift).astype(BF16)
        accs = []
        for gi, (c0, cn) in enumerate(streams[st][1]):
            acc = jnp.dot(v_augs[gi], p[:, c0:c0 + cn], preferred_element_type=F32)
            if prev is not None:
                acc = (prev[gi] if alpha is None else prev[gi] * alpha[:, c0:c0 + cn]) + acc
            accs.append(acc)
        carries[st] = (shift, m_seen, accs)

    outs = [[acc[:HEAD_DIM] / acc[HEAD_DIM:HEAD_DIM + 1] for acc in accs] for _, _, accs in carries]
    excess = functools.reduce(jnp.maximum, [
        (m_seen - shift)[:, c:c + LANES]
        for shift, m_seen, _ in carries for c in range(0, shift.shape[1], LANES)])
    return outs, excess


def _masked_heads(qt):
    lo_rows = lax.broadcasted_iota(jnp.int32, qt.shape, 0) < HEAD_DIM
    zero = jnp.zeros_like(qt)
    return jnp.where(lo_rows, qt, zero), jnp.where(lo_rows, zero, qt)


def _split_vt(v):
    vt = v.T
    return [vt[:HEAD_DIM], vt[HEAD_DIM:]]


EXCESS_TILE = (8, LANES)


def _store_tiles(o_ref, ex_ref, tiles, excess):
    for p, o in enumerate(tiles):
        o_ref[0, :, p * LANES:(p + 1) * LANES] = o.astype(o_ref.dtype)
    if ex_ref is not None:
        ex_ref[0, 0, 0] = jnp.broadcast_to(excess, EXCESS_TILE)


def _guarded(call):
    out, excess = call(False)
    return lax.cond(jnp.max(excess) <= MAX_SHIFT_EXCESS, lambda: out, lambda: call(True)[0])


def _attn_shared_kernel(q_ref, *refs, chunks, online, guard):
    o_ref, ex_ref = (refs[-2], refs[-1]) if guard else (refs[-1], None)
    kv = refs[:2 * len(chunks)]
    npairs, tq = q_ref.shape[1], q_ref.shape[3]
    kv_sets = [(kv[2 * i].shape[1], chunk,
                lambda st, start, size, r=kv[2 * i]: r[0, start:start + size, :],
                lambda st, start, size, r=kv[2 * i + 1]: [r[0, 0, :, start:start + size]])
               for i, chunk in enumerate(chunks)]
    masked = [_masked_heads(q_ref[0, p]) for p in range(npairs)]
    q_t = jnp.concatenate([m[0] for m in masked] + [m[1] for m in masked], axis=1)
    outs, excess = _flash_t([(q_t, [(0, 2 * npairs * tq)])], kv_sets, online)
    o_t = outs[0][0]
    tiles = [jnp.concatenate([o_t[:, p * tq:(p + 1) * tq],
                              o_t[:, (npairs + p) * tq:(npairs + p + 1) * tq]], axis=0).T
             for p in range(npairs)]
    _store_tiles(o_ref, ex_ref, tiles, excess)


def _attention_outputs(out_spec, out_sds, grid, guard):
    if not guard:
        return out_spec, out_sds
    ex_spec = pl.BlockSpec((1, 1, 1) + EXCESS_TILE, lambda i, c, j: (i, c, j, 0, 0))
    return [out_spec, ex_spec], [out_sds, jax.ShapeDtypeStruct(tuple(grid) + EXCESS_TILE, F32)]


def _shared_attention(qt, kv_sets, tq, chunks):
    b, _, _, t = qt.shape
    pairs_per_step = PAIRS // A_KV_HEADS
    in_specs = [pl.BlockSpec((1, pairs_per_step, LANES, tq), lambda i, c, j: (i, c, 0, j))]
    args = [qt]
    for k, vt in kv_sets:
        nk = k.shape[1]
        in_specs += [pl.BlockSpec((1, nk, LANES), lambda i, c, j: (i, 0, c)),
                     pl.BlockSpec((1, 1, HEAD_DIM, nk), lambda i, c, j: (i, c, 0, 0))]
        args += [k, vt]
    grid = (b, A_KV_HEADS, t // tq)
    guard = sum(k.shape[1] // chunk for (k, _), chunk in zip(kv_sets, chunks)) > 1
    out_specs, out_shape = _attention_outputs(
        pl.BlockSpec((1, tq, pairs_per_step * LANES), lambda i, c, j: (i, j, c)),
        jax.ShapeDtypeStruct((b, t, A_WIDTH), BF16), grid, guard)

    def call(online):
        return pl.pallas_call(
            functools.partial(_attn_shared_kernel, chunks=tuple(chunks), online=online, guard=guard),
            grid=grid, in_specs=in_specs, out_specs=out_specs, out_shape=out_shape,
            compiler_params=pltpu.CompilerParams(dimension_semantics=("arbitrary",) * 3,
                                                 vmem_limit_bytes=VMEM_LIMIT),
            name="shared_attention_online" if online else "shared_attention",
        )(*args)

    return _guarded(call) if guard else call(False)


def _pair_streams(q_ref):
    tq = q_ref.shape[3]
    return [(jnp.concatenate(_masked_heads(q_ref[0, p]), axis=1), [(0, tq), (tq, tq)])
            for p in range(q_ref.shape[1])]


def _pair_tiles(outs):
    return [jnp.concatenate(groups, axis=0).T for groups in outs]


def _pair_loaders(k_ref, v_ref, rows):
    lanes = lambda st: slice(st * LANES, (st + 1) * LANES)
    return (lambda st, start, size: k_ref[0, rows(start, size), lanes(st)],
            lambda st, start, size: _split_vt(v_ref[0, rows(start, size), lanes(st)]))


def _attn_full_kernel(q_ref, k_ref, v_ref, o_ref):
    nk = k_ref.shape[1]
    kv_sets = [(nk, nk) + _pair_loaders(k_ref, v_ref, lambda start, size: slice(start, start + size))]
    outs, excess = _flash_t(_pair_streams(q_ref), kv_sets, False)
    _store_tiles(o_ref, None, _pair_tiles(outs), excess)


def _full_attention(qt, k, v):
    b, npairs, _, t = qt.shape
    tok = pl.BlockSpec((1, t, npairs * LANES), lambda i: (i, 0, 0))
    return pl.pallas_call(
        _attn_full_kernel,
        grid=(b,),
        in_specs=[pl.BlockSpec((1, npairs, LANES, t), lambda i: (i, 0, 0, 0)), tok, tok],
        out_specs=tok,
        out_shape=jax.ShapeDtypeStruct((b, t, npairs * LANES), BF16),
        compiler_params=pltpu.CompilerParams(dimension_semantics=("arbitrary",),
                                             vmem_limit_bytes=VMEM_LIMIT),
        name="full_attention",
    )(qt, k, v)


NAT_PAIRS_PER_STEP = 4


def _attn_nat_kernel(q_ref, kw_ref, vw_ref, kc_ref, vc_ref, bias_ref, o_ref, ex_ref, *, rows, online):
    j = pl.program_id(2)
    ws = jnp.clip(j * NAT_Q_ROWS - WIN_H // 2, 0, rows - NAT_K_ROWS)
    win = pl.multiple_of(ws * GRID_W, GRID_W)
    nk = NAT_K_ROWS * GRID_W
    nc = kc_ref.shape[1]
    kv_sets = [(nc, nc) + _pair_loaders(kc_ref, vc_ref, lambda start, size: slice(start, start + size)),
               (nk, nk) + _pair_loaders(kw_ref, vw_ref, lambda start, size: pl.ds(win, nk))]
    load_bias = lambda st: jnp.concatenate([bias_ref[0, 2 * st], bias_ref[0, 2 * st + 1]], axis=1)

    outs, excess = _flash_t(_pair_streams(q_ref), kv_sets, online, bias=(1, load_bias))
    _store_tiles(o_ref, ex_ref, _pair_tiles(outs), excess)


def _nat_attention(qt, k, v, ctx_k, ctx_v, bias):
    b, _, _, t = qt.shape
    rows = t // GRID_W
    nblk = rows // NAT_Q_ROWS
    tq = NAT_Q_ROWS * GRID_W
    nk = NAT_K_ROWS * GRID_W
    width = NAT_PAIRS_PER_STEP * LANES
    whole = lambda n: pl.BlockSpec((1, n, width), lambda i, c, j: (i, 0, c))

    def bias_map(i, c, j):
        variant = jnp.where(j == 0, 0, jnp.where(j == nblk - 1, 2, 1))
        return (variant, c, 0, 0)

    grid = (b, PAIRS // NAT_PAIRS_PER_STEP, nblk)
    out_specs, out_shape = _attention_outputs(
        pl.BlockSpec((1, tq, width), lambda i, c, j: (i, j, c)),
        jax.ShapeDtypeStruct((b, t, B_WIDTH), BF16), grid, True)

    def call(online):
        return pl.pallas_call(
            functools.partial(_attn_nat_kernel, rows=rows, online=online),
            grid=grid,
            in_specs=[pl.BlockSpec((1, NAT_PAIRS_PER_STEP, LANES, tq), lambda i, c, j: (i, c, 0, j)),
                      whole(t), whole(t), whole(ctx_k.shape[1]), whole(ctx_v.shape[1]),
                      pl.BlockSpec((1, 2 * NAT_PAIRS_PER_STEP, nk, tq), bias_map)],
            out_specs=out_specs, out_shape=out_shape,
            compiler_params=pltpu.CompilerParams(dimension_semantics=("arbitrary",) * 3,
                                                 vmem_limit_bytes=VMEM_LIMIT),
            name="nat_attention_online" if online else "nat_attention",
        )(qt, k, v, ctx_k, ctx_v, bias)

    return _guarded(call)


def _post_kernel(x_ref, ao_ref, bo_ref, mod_ref, n1_ref, n2_ref, fg_ref,
                 wg_ref, wa_ref, wb_ref, wo_ref, w1_ref, w2_ref, y_ref):
    x = x_ref[0]
    mod = mod_ref[0]
    h = _rms_mod(x, n1_ref[...], mod[1:2], mod[0:1])
    gates = jnp.dot(h.astype(BF16), wg_ref[...], preferred_element_type=F32)
    ma = jnp.dot(ao_ref[0], wa_ref[...], preferred_element_type=F32)
    mb = jnp.dot(bo_ref[0], wb_ref[...], preferred_element_type=F32)
    m = jax.nn.sigmoid(gates[:, :D_MODEL]) * ma + jax.nn.sigmoid(gates[:, D_MODEL:]) * mb
    x1 = x + mod[2:3] * jnp.dot(m.astype(BF16), wo_ref[...], preferred_element_type=F32)
    h2 = _rms_mod(x1, n2_ref[...], mod[4:5], mod[3:4])
    u = jnp.maximum(jnp.dot(h2.astype(BF16), w1_ref[...], preferred_element_type=F32), 0.0)
    x2 = x1 + mod[5:6] * jnp.dot((u * u).astype(BF16), w2_ref[...], preferred_element_type=F32)
    inv = lax.rsqrt(jnp.mean(x2 * x2, axis=-1, keepdims=True) + RMS_EPS)
    y_ref[0] = (x2 * inv) * fg_ref[...]


def _post_attention(x, a_o, b_o, mod, mod_row, n1, n2, fg, wg, wa, wb, wo, w1, w2, tm):
    b, t, d = x.shape
    tok = lambda w: pl.BlockSpec((1, tm, w), lambda i, j: (i, j, 0))
    weights = [wg, wa, wb, wo, w1, w2]
    return pl.pallas_call(
        _post_kernel,
        grid=(b, t // tm),
        in_specs=[tok(d), tok(A_WIDTH), tok(B_WIDTH),
                  pl.BlockSpec((1, N_MOD, d), lambda i, j: (mod_row(i), 0, 0)),
                  _const_spec((1, d)), _const_spec((1, d)), _const_spec((1, d))]
                 + [_const_spec(w.shape) for w in weights],
        out_specs=tok(d),
        out_shape=jax.ShapeDtypeStruct(x.shape, F32),
        compiler_params=pltpu.CompilerParams(dimension_semantics=("arbitrary", "arbitrary"),
                                             vmem_limit_bytes=VMEM_LIMIT),
        name="post_attention",
    )(x, a_o, b_o, mod, n1, n2, fg, *weights)


def _rope_tables(t):
    pos = jnp.arange(t, dtype=jnp.int32)
    row = (pos // GRID_W).astype(F32)
    col = (pos % GRID_W).astype(F32)
    inv = ROPE_BASE ** (-jnp.arange(ROPE_PAIRS, dtype=F32) / ROPE_PAIRS)
    ar = row[:, None] * inv
    ac = col[:, None] * inv
    cos = jnp.concatenate([jnp.cos(ar), jnp.cos(ar), jnp.cos(ac), jnp.cos(ac)], axis=-1)
    sin = jnp.concatenate([-jnp.sin(ar), jnp.sin(ar), -jnp.sin(ac), jnp.sin(ac)], axis=-1)
    return jnp.tile(cos, (1, LANES // HEAD_DIM)), jnp.tile(sin, (1, LANES // HEAD_DIM))


def _dup_heads(x):
    b, t, kv, dh = x.shape
    return jnp.broadcast_to(x[:, :, :, None, :], (b, t, kv, 2, dh)).reshape(b, t, kv * 2 * dh).astype(BF16)


def kernel(x_prompt, x_sample, cache_a_k, cache_a_v, cache_b_k, cache_b_v, c, c_ctx, w_mod, b_mod,
           norm1_g, norm2_g, w_in, q_norm_g, k_norm_g, nat_bias, w_br_a, w_br_b, w_out, w_mlp_in,
           w_mlp_out, final_norm_g):
    assert w_mod.shape[0] == 1, "single-layer trunk"
    nb, seq, d = x_prompt.shape
    nd, dseq, _ = x_sample.shape
    past = cache_a_k.shape[2]

    ctx_row = nd
    pad = (-(nd + 1)) % 8
    cvec = jnp.concatenate([c, c_ctx[None, :], jnp.zeros((pad, d), F32)], axis=0)
    mod = _modulation(cvec, w_mod[0], b_mod[0]).reshape(cvec.shape[0], N_MOD, d)

    w_qkv = w_in[0, :, :QKV_WIDTH].astype(BF16)
    w_gate = w_in[0, :, QKV_WIDTH:].astype(BF16)
    wa = w_br_a[0].astype(BF16)
    wb = w_br_b[0].astype(BF16)
    wo = w_out[0].astype(BF16)
    w1 = w_mlp_in[0].astype(BF16)
    w2 = w_mlp_out[0].astype(BF16)
    n1 = norm1_g[0].reshape(1, d)
    n2 = norm2_g[0].reshape(1, d)
    fg = final_norm_g.reshape(1, d)
    qg = (jnp.tile(q_norm_g[0], A_HEADS) * Q_SCALE).reshape(1, A_WIDTH)
    kg = jnp.tile(k_norm_g[0], A_KV_HEADS).reshape(1, A_KV_WIDTH)
    head_id = np.arange(LANES) // HEAD_DIM
    hm = jnp.asarray(head_id[:, None] == head_id[None, :], dtype=BF16)

    bias = _expand_bias(nat_bias[0])

    ctx_mod_row = lambda i: ctx_row
    (qa, ka, va, qb, kb, vb, new_ak, new_av, new_bk, new_bv) = _pre_attention(
        x_prompt, mod, ctx_mod_row, n1, w_qkv, qg, kg, hm, None, True, seq)
    a_o = _shared_attention(qa, [(ka, va)], seq, [seq])
    b_o = _full_attention(qb, kb, vb)
    y_prompt = _post_attention(x_prompt, a_o, b_o, mod, ctx_mod_row, n1, n2, fg,
                               w_gate, wa, wb, wo, w1, w2, seq)

    lat_mod_row = lambda i: i
    (qa, ka, va, qb, kb, vb) = _pre_attention(
        x_sample, mod, lat_mod_row, n1, w_qkv, qg, kg, hm, _rope_tables(dseq), False, 512)
    ca_k = _dup_heads(cache_a_k[:, 0])
    ca_vt = jnp.transpose(cache_a_v[:, 0], (0, 2, 3, 1)).astype(BF16)
    cb_k = cache_b_k[:, 0].reshape(nd, past, B_WIDTH).astype(BF16)
    cb_v = cache_b_v[:, 0].reshape(nd, past, B_WIDTH).astype(BF16)
    a_o = _shared_attention(qa, [(ca_k, ca_vt), (ka, va)], 512, [256, 256])
    b_o = _nat_attention(qb, kb, vb, cb_k, cb_v, bias)
    y_sample = _post_attention(x_sample, a_o, b_o, mod, lat_mod_row, n1, n2, fg,
                               w_gate, wa, wb, wo, w1, w2, 256)

    return (y_prompt, y_sample,
            new_ak.reshape(nb, 1, seq, A_KV_HEADS, HEAD_DIM),
            new_av.reshape(nb, 1, seq, A_KV_HEADS, HEAD_DIM),
            new_bk.reshape(nb, 1, seq, B_HEADS, HEAD_DIM),
            new_bv.reshape(nb, 1, seq, B_HEADS, HEAD_DIM))
```

```python
import functools

import numpy as np
import jax
import jax.numpy as jnp
from jax import lax
from jax.experimental import pallas as pl
from jax.experimental.pallas import tpu as pltpu

F32 = jnp.float32
BF16 = jnp.bfloat16

D_MODEL = 1024
GRID_W = 64
HEAD_DIM = 64
A_HEADS = 8
A_KV_HEADS = 2
B_HEADS = 8
A_WIDTH = A_HEADS * HEAD_DIM
A_KV_WIDTH = A_KV_HEADS * HEAD_DIM
B_WIDTH = B_HEADS * HEAD_DIM
D_FF = 4 * D_MODEL
WIN_H = 8
WIN_W = 16
ROPE_BASE = 10000.0
ROPE_PAIRS = HEAD_DIM // 4
RMS_EPS = 1e-6
N_MOD = 6
NEG_INF = -1e30

LANES = 128
PAIRS = A_WIDTH // LANES
QKV_WIDTH = A_WIDTH + 2 * A_KV_WIDTH + 3 * B_WIDTH
GATE_WIDTH = 2 * D_MODEL
NAT_Q_ROWS = 4
NAT_K_ROWS = 12
PRE_SUB_ROWS = 256
VMEM_LIMIT = 56 * 1024 * 1024
LOG2E = 1.4426950408889634
Q_SCALE = HEAD_DIM ** -0.5 * LOG2E


def _const_spec(shape):
    nd = len(shape)
    return pl.BlockSpec(shape, lambda *_: (0,) * nd, pipeline_mode=pl.Buffered(1))


def _rms_mod(x, g, scale, shift):
    inv = lax.rsqrt(jnp.mean(x * x, axis=-1, keepdims=True) + RMS_EPS)
    return (x * inv) * g * (1.0 + scale) + shift


def _mod_kernel(c_ref, w_ref, b_ref, o_ref):
    cv = c_ref[...]
    s = cv * jax.nn.sigmoid(cv)
    o_ref[...] = jnp.dot(s.astype(BF16), w_ref[...].astype(BF16),
                         preferred_element_type=F32) + b_ref[...]


def _modulation(cvec, w_mod, b_mod):
    n, d = cvec.shape
    width = w_mod.shape[1]
    tn = 1024
    return pl.pallas_call(
        _mod_kernel,
        grid=(width // tn,),
        in_specs=[pl.BlockSpec((n, d), lambda j: (0, 0)),
                  pl.BlockSpec((d, tn), lambda j: (0, j)),
                  pl.BlockSpec((1, tn), lambda j: (0, j))],
        out_specs=pl.BlockSpec((n, tn), lambda j: (0, j)),
        out_shape=jax.ShapeDtypeStruct((n, width), F32),
        compiler_params=pltpu.CompilerParams(dimension_semantics=("arbitrary",),
                                             vmem_limit_bytes=VMEM_LIMIT),
        name="modulation",
    )(cvec, w_mod, b_mod.reshape(1, width))


def _nat_variant_geometry(variant):
    if variant == 0:
        return 0, (lambda i, n: n < WIN_H)
    if variant == 1:
        return -(WIN_H // 2), (lambda i, n: 0 <= n - i < WIN_H)
    return -WIN_H, (lambda i, n: n >= NAT_K_ROWS - WIN_H)


def _bias_kernel(r_ref, o_ref):
    h = pl.program_id(0)
    n_dr = 2 * WIN_H - 1
    n_dc = 2 * WIN_W - 1
    kc = lax.broadcasted_iota(jnp.int32, (GRID_W, LANES), 0)
    lane = lax.broadcasted_iota(jnp.int32, (GRID_W, LANES), 1)
    qc = lane & (GRID_W - 1)
    hi_half = lane >= GRID_W
    dc = kc - qc + (WIN_W - 1)
    cstart = jnp.clip(qc - WIN_W // 2, 0, GRID_W - WIN_W)
    col_ok = (kc >= cstart) & (kc < cstart + WIN_W)
    neg = jnp.full((GRID_W, LANES), NEG_INF, F32)
    base = h * (n_dr * n_dc)
    tiles = []
    for a in range(n_dr):
        val = neg
        for b in range(n_dc):
            val = jnp.where(dc == b, r_ref[base + a * n_dc + b] * LOG2E, val)
        tiles.append(jnp.where(col_ok, val, neg))
    for variant in range(3):
        off, valid = _nat_variant_geometry(variant)
        for n in range(NAT_K_ROWS):
            for p in range(NAT_Q_ROWS // 2):
                halves = []
                for i in (2 * p, 2 * p + 1):
                    a = n - i + off + WIN_H - 1
                    halves.append(tiles[a] if valid(i, n) else neg)
                o_ref[variant, 0, n * GRID_W:(n + 1) * GRID_W, p * LANES:(p + 1) * LANES] = (
                    jnp.where(hi_half, halves[1], halves[0]))


def _expand_bias(nat_bias):
    heads = nat_bias.shape[0]
    nq = NAT_Q_ROWS * GRID_W
    nk = NAT_K_ROWS * GRID_W
    return pl.pallas_call(
        _bias_kernel,
        grid=(heads,),
        in_specs=[pl.BlockSpec(memory_space=pltpu.SMEM)],
        out_specs=pl.BlockSpec((3, 1, nk, nq), lambda h: (0, h, 0, 0)),
        out_shape=jax.ShapeDtypeStruct((3, heads, nk, nq), F32),
        compiler_params=pltpu.CompilerParams(dimension_semantics=("arbitrary",),
                                             vmem_limit_bytes=VMEM_LIMIT),
        name="bias_expand",
    )(nat_bias.reshape(-1))


def _pre_kernel(*refs, rope, emit_cache):
    x_ref, mod_ref, n1_ref, w_ref, qg_ref, kg_ref, hm_ref = refs[:7]
    pos = 7
    if rope:
        cos_ref, sin_ref = refs[pos:pos + 2]
        pos += 2
    qa_ref, ka_ref, va_ref, qb_ref, kb_ref, vb_ref = refs[pos:pos + 6]
    pos += 6
    if emit_cache:
        cak_ref, cav_ref, cbk_ref, cbv_ref = refs[pos:pos + 4]

    mod = mod_ref[0]
    tm = x_ref.shape[1]
    sub = min(tm, PRE_SUB_ROWS)
    row_slices = [slice(r, r + sub) for r in range(0, tm, sub)]
    hs = [_rms_mod(x_ref[0, rows], n1_ref[...], mod[1:2], mod[0:1]).astype(BF16) for rows in row_slices]
    accs = [jnp.dot(h, w_ref[...], preferred_element_type=F32) for h in hs]
    lane = lax.broadcasted_iota(jnp.int32, (sub, LANES), 1)
    lo_half = lane < HEAD_DIM
    hm = hm_ref[...]

    def head_norm(y, g):
        ms = jnp.dot((y * y).astype(BF16), hm, preferred_element_type=F32) * (1.0 / HEAD_DIM)
        return (y * lax.rsqrt(ms + RMS_EPS)) * g

    for rows, acc in zip(row_slices, accs):
        def rotary(y):
            partner = jnp.where((lane & ROPE_PAIRS) == 0,
                                pltpu.roll(y, LANES - ROPE_PAIRS, 1), pltpu.roll(y, ROPE_PAIRS, 1))
            return y * cos_ref[rows] + partner * sin_ref[rows]

        def store_dup(ref, y):
            r = pltpu.roll(y, HEAD_DIM, 1)
            ref[0, rows, 0:LANES] = jnp.where(lo_half, y, r).astype(ref.dtype)
            ref[0, rows, LANES:2 * LANES] = jnp.where(lo_half, r, y).astype(ref.dtype)

        for c in range(PAIRS):
            sl = slice(c * LANES, (c + 1) * LANES)
            y = head_norm(acc[:, sl], qg_ref[:, sl])
            if rope:
                y = rotary(y)
            qa_ref[0, c, :, rows] = y.T.astype(qa_ref.dtype)

        k = head_norm(acc[:, A_WIDTH:A_WIDTH + A_KV_WIDTH], kg_ref[...])
        v = acc[:, A_WIDTH + A_KV_WIDTH:A_WIDTH + 2 * A_KV_WIDTH]
        if emit_cache:
            cak_ref[0, rows] = k
            cav_ref[0, rows] = v
        if rope:
            k = rotary(k)
        store_dup(ka_ref, k)
        vt = v.T
        for g in range(A_KV_HEADS):
            va_ref[0, g, :, rows] = vt[g * HEAD_DIM:(g + 1) * HEAD_DIM].astype(va_ref.dtype)

        b0 = A_WIDTH + 2 * A_KV_WIDTH
        bq = acc[:, b0:b0 + B_WIDTH]
        bk = acc[:, b0 + B_WIDTH:b0 + 2 * B_WIDTH]
        bv = acc[:, b0 + 2 * B_WIDTH:b0 + 3 * B_WIDTH]
        for c in range(PAIRS):
            qb_ref[0, c, :, rows] = (bq[:, c * LANES:(c + 1) * LANES] * Q_SCALE).T.astype(qb_ref.dtype)
        kb_ref[0, rows] = bk.astype(kb_ref.dtype)
        vb_ref[0, rows] = bv.astype(vb_ref.dtype)
        if emit_cache:
            cbk_ref[0, rows] = bk
            cbv_ref[0, rows] = bv


def _pre_attention(x, mod, mod_row, n1, w_qkv, qg, kg, hm, rope_tables, emit_cache, tm):
    b, t, d = x.shape
    rope = rope_tables is not None
    tok = lambda w: pl.BlockSpec((1, tm, w), lambda i, j: (i, j, 0))
    in_specs = [tok(d),
                pl.BlockSpec((1, N_MOD, d), lambda i, j: (mod_row(i), 0, 0)),
                _const_spec((1, d)), _const_spec(w_qkv.shape), _const_spec(qg.shape),
                _const_spec(kg.shape), _const_spec(hm.shape)]
    args = [x, mod, n1, w_qkv, qg, kg, hm]
    if rope:
        in_specs += [pl.BlockSpec((tm, LANES), lambda i, j: (j, 0))] * 2
        args += list(rope_tables)
    qt = (PAIRS, LANES)
    vt = (A_KV_HEADS, HEAD_DIM)
    layouts = [qt, 2 * LANES, vt, qt, B_WIDTH, B_WIDTH]
    out_specs = [tok(w) if isinstance(w, int) else
                 pl.BlockSpec((1,) + w + (tm,), lambda i, j: (i, 0, 0, j)) for w in layouts]
    out_shape = [jax.ShapeDtypeStruct((b, t, w) if isinstance(w, int) else (b,) + w + (t,), BF16)
                 for w in layouts]
    if emit_cache:
        cw = [A_KV_WIDTH, A_KV_WIDTH, B_WIDTH, B_WIDTH]
        out_specs += [tok(w) for w in cw]
        out_shape += [jax.ShapeDtypeStruct((b, t, w), F32) for w in cw]
    return pl.pallas_call(
        functools.partial(_pre_kernel, rope=rope, emit_cache=emit_cache),
        grid=(b, t // tm),
        in_specs=in_specs, out_specs=out_specs, out_shape=out_shape,
        compiler_params=pltpu.CompilerParams(dimension_semantics=("arbitrary", "arbitrary"),
                                             vmem_limit_bytes=VMEM_LIMIT),
        name="pre_attention",
    )(*args)


ONES_ROWS = 16


def _flash_t(streams, kv_sets, online, bias=None):
    items = [(si, start, chunk, st)
             for si, (n_keys, chunk, _, _) in enumerate(kv_sets)
             for start in range(0, n_keys, chunk)
             for st in range(len(streams))]

    def scores(item):
        si, start, size, st = item
        k = kv_sets[si][2](st, start, size)
        s = jnp.dot(k, streams[st][0], preferred_element_type=F32)
        if bias is not None and si == bias[0]:
            s = s + bias[1](st)
        return s

    carries = [None] * len(streams)
    pending = None
    for idx, item in enumerate(items):
        s = pending if pending is not None else scores(item)
        pending = None
        si, start, size, st = item
        if (online or carries[st] is None) and idx + 1 < len(items):
            pending = scores(items[idx + 1])
        ones = jnp.ones((ONES_ROWS, size), BF16)
        v_augs = [jnp.concatenate([vt, ones], axis=0) for vt in kv_sets[si][3](st, start, size)]
        alpha = None
        if carries[st] is None:
            shift, prev = jnp.max(s, axis=0, keepdims=True), None
        elif online:
            m_prev, prev = carries[st]
            shift = jnp.maximum(m_prev, jnp.max(s, axis=0, keepdims=True))
            alpha = jnp.exp2(m_prev - shift)
        else:
            shift, prev = carries[st]
        p = jnp.exp2(s - shift).astype(BF16)
        accs = []
        for gi, (c0, cn) in enumerate(streams[st][1]):
            acc = jnp.dot(v_augs[gi], p[:, c0:c0 + cn], preferred_element_type=F32)
            if prev is not None:
                acc = (prev[gi] if alpha is None else prev[gi] * alpha[:, c0:c0 + cn]) + acc
            accs.append(acc)
        carries[st] = (shift, accs)

    outs = [[acc[:HEAD_DIM] / acc[HEAD_DIM:HEAD_DIM + 1] for acc in accs] for _, accs in carries]
    flags = []
    for _, accs in carries:
        for acc in accs:
            bad = jnp.max(jnp.where(jnp.isfinite(acc), 0.0, 1.0), axis=0, keepdims=True)
            flags += [bad[:, c:c + LANES] for c in range(0, bad.shape[1], LANES)]
    return outs, functools.reduce(jnp.maximum, flags)


def _masked_heads(qt):
    lo_rows = lax.broadcasted_iota(jnp.int32, qt.shape, 0) < HEAD_DIM
    zero = jnp.zeros_like(qt)
    return jnp.where(lo_rows, qt, zero), jnp.where(lo_rows, zero, qt)


def _split_vt(v):
    vt = v.T
    return [vt[:HEAD_DIM], vt[HEAD_DIM:]]


EXCESS_TILE = (8, LANES)


def _store_tiles(o_ref, ex_ref, tiles, overflow):
    for p, o in enumerate(tiles):
        o_ref[0, :, p * LANES:(p + 1) * LANES] = o.astype(o_ref.dtype)
    if ex_ref is not None:
        ex_ref[0, 0, 0] = jnp.broadcast_to(overflow, EXCESS_TILE)


def _guarded(call):
    out, overflow = call(False)
    return lax.cond(jnp.max(overflow) == 0.0, lambda: out, lambda: call(True)[0])


def _attn_shared_kernel(q_ref, *refs, chunks, online, guard):
    o_ref, ex_ref = (refs[-2], refs[-1]) if guard else (refs[-1], None)
    kv = refs[:2 * len(chunks)]
    npairs, tq = q_ref.shape[1], q_ref.shape[3]
    kv_sets = [(kv[2 * i].shape[1], chunk,
                lambda st, start, size, r=kv[2 * i]: r[0, start:start + size, :],
                lambda st, start, size, r=kv[2 * i + 1]: [r[0, 0, :, start:start + size]])
               for i, chunk in enumerate(chunks)]
    masked = [_masked_heads(q_ref[0, p]) for p in range(npairs)]
    q_t = jnp.concatenate([m[0] for m in masked] + [m[1] for m in masked], axis=1)
    outs, overflow = _flash_t([(q_t, [(0, 2 * npairs * tq)])], kv_sets, online)
    o_t = outs[0][0]
    tiles = [jnp.concatenate([o_t[:, p * tq:(p + 1) * tq],
                              o_t[:, (npairs + p) * tq:(npairs + p + 1) * tq]], axis=0).T
             for p in range(npairs)]
    _store_tiles(o_ref, ex_ref, tiles, overflow)


def _attention_outputs(out_spec, out_sds, grid, guard):
    if not guard:
        return out_spec, out_sds
    ex_spec = pl.BlockSpec((1, 1, 1) + EXCESS_TILE, lambda i, c, j: (i, c, j, 0, 0))
    return [out_spec, ex_spec], [out_sds, jax.ShapeDtypeStruct(tuple(grid) + EXCESS_TILE, F32)]


def _shared_attention(qt, kv_sets, tq, chunks):
    b, _, _, t = qt.shape
    pairs_per_step = PAIRS // A_KV_HEADS
    in_specs = [pl.BlockSpec((1, pairs_per_step, LANES, tq), lambda i, c, j: (i, c, 0, j))]
    args = [qt]
    for k, vt in kv_sets:
        nk = k.shape[1]
        in_specs += [pl.BlockSpec((1, nk, LANES), lambda i, c, j: (i, 0, c)),
                     pl.BlockSpec((1, 1, HEAD_DIM, nk), lambda i, c, j: (i, c, 0, 0))]
        args += [k, vt]
    grid = (b, A_KV_HEADS, t // tq)
    guard = sum(k.shape[1] // chunk for (k, _), chunk in zip(kv_sets, chunks)) > 1
    out_specs, out_shape = _attention_outputs(
        pl.BlockSpec((1, tq, pairs_per_step * LANES), lambda i, c, j: (i, j, c)),
        jax.ShapeDtypeStruct((b, t, A_WIDTH), BF16), grid, guard)

    def call(online):
        return pl.pallas_call(
            functools.partial(_attn_shared_kernel, chunks=tuple(chunks), online=online, guard=guard),
            grid=grid, in_specs=in_specs, out_specs=out_specs, out_shape=out_shape,
            compiler_params=pltpu.CompilerParams(dimension_semantics=("arbitrary",) * 3,
                                                 vmem_limit_bytes=VMEM_LIMIT),
            name="shared_attention_online" if online else "shared_attention",
        )(*args)

    return _guarded(call) if guard else call(False)


def _pair_streams(q_ref):
    tq = q_ref.shape[3]
    return [(jnp.concatenate(_masked_heads(q_ref[0, p]), axis=1), [(0, tq), (tq, tq)])
            for p in range(q_ref.shape[1])]


def _pair_tiles(outs):
    return [jnp.concatenate(groups, axis=0).T for groups in outs]


def _pair_loaders(k_ref, v_ref, rows):
    lanes = lambda st: slice(st * LANES, (st + 1) * LANES)
    return (lambda st, start, size: k_ref[0, rows(start, size), lanes(st)],
            lambda st, start, size: _split_vt(v_ref[0, rows(start, size), lanes(st)]))


def _attn_full_kernel(q_ref, k_ref, v_ref, o_ref):
    nk = k_ref.shape[1]
    kv_sets = [(nk, nk) + _pair_loaders(k_ref, v_ref, lambda start, size: slice(start, start + size))]
    outs, overflow = _flash_t(_pair_streams(q_ref), kv_sets, False)
    _store_tiles(o_ref, None, _pair_tiles(outs), overflow)


def _full_attention(qt, k, v):
    b, npairs, _, t = qt.shape
    tok = pl.BlockSpec((1, t, npairs * LANES), lambda i: (i, 0, 0))
    return pl.pallas_call(
        _attn_full_kernel,
        grid=(b,),
        in_specs=[pl.BlockSpec((1, npairs, LANES, t), lambda i: (i, 0, 0, 0)), tok, tok],
        out_specs=tok,
        out_shape=jax.ShapeDtypeStruct((b, t, npairs * LANES), BF16),
        compiler_params=pltpu.CompilerParams(dimension_semantics=("arbitrary",),
                                             vmem_limit_bytes=VMEM_LIMIT),
        name="full_attention",
    )(qt, k, v)


NAT_PAIRS_PER_STEP = 4


def _attn_nat_kernel(q_ref, kw_ref, vw_ref, kc_ref, vc_ref, bias_ref, o_ref, ex_ref, *, rows, online):
    j = pl.program_id(2)
    ws = jnp.clip(j * NAT_Q_ROWS - WIN_H // 2, 0, rows - NAT_K_ROWS)
    win = pl.multiple_of(ws * GRID_W, GRID_W)
    nk = NAT_K_ROWS * GRID_W
    nc = kc_ref.shape[1]
    kv_sets = [(nc, nc) + _pair_loaders(kc_ref, vc_ref, lambda start, size: slice(start, start + size)),
               (nk, nk) + _pair_loaders(kw_ref, vw_ref, lambda start, size: pl.ds(win, nk))]
    load_bias = lambda st: jnp.concatenate([bias_ref[0, 2 * st], bias_ref[0, 2 * st + 1]], axis=1)

    outs, overflow = _flash_t(_pair_streams(q_ref), kv_sets, online, bias=(1, load_bias))
    _store_tiles(o_ref, ex_ref, _pair_tiles(outs), overflow)


def _nat_attention(qt, k, v, ctx_k, ctx_v, bias):
    b, _, _, t = qt.shape
    rows = t // GRID_W
    nblk = rows // NAT_Q_ROWS
    tq = NAT_Q_ROWS * GRID_W
    nk = NAT_K_ROWS * GRID_W
    width = NAT_PAIRS_PER_STEP * LANES
    whole = lambda n: pl.BlockSpec((1, n, width), lambda i, c, j: (i, 0, c))

    def bias_map(i, c, j):
        variant = jnp.where(j == 0, 0, jnp.where(j == nblk - 1, 2, 1))
        return (variant, c, 0, 0)

    grid = (b, PAIRS // NAT_PAIRS_PER_STEP, nblk)
    out_specs, out_shape = _attention_outputs(
        pl.BlockSpec((1, tq, width), lambda i, c, j: (i, j, c)),
        jax.ShapeDtypeStruct((b, t, B_WIDTH), BF16), grid, True)

    def call(online):
        return pl.pallas_call(
            functools.partial(_attn_nat_kernel, rows=rows, online=online),
            grid=grid,
            in_specs=[pl.BlockSpec((1, NAT_PAIRS_PER_STEP, LANES, tq), lambda i, c, j: (i, c, 0, j)),
                      whole(t), whole(t), whole(ctx_k.shape[1]), whole(ctx_v.shape[1]),
                      pl.BlockSpec((1, 2 * NAT_PAIRS_PER_STEP, nk, tq), bias_map)],
            out_specs=out_specs, out_shape=out_shape,
            compiler_params=pltpu.CompilerParams(dimension_semantics=("arbitrary",) * 3,
                                                 vmem_limit_bytes=VMEM_LIMIT),
            name="nat_attention_online" if online else "nat_attention",
        )(qt, k, v, ctx_k, ctx_v, bias)

    return _guarded(call)


def _post_kernel(x_ref, ao_ref, bo_ref, mod_ref, n1_ref, n2_ref, fg_ref,
                 wg_ref, wa_ref, wb_ref, wo_ref, w1_ref, w2_ref, y_ref):
    x = x_ref[0]
    mod = mod_ref[0]
    h = _rms_mod(x, n1_ref[...], mod[1:2], mod[0:1])
    gates = jnp.dot(h.astype(BF16), wg_ref[...], preferred_element_type=F32)
    ma = jnp.dot(ao_ref[0], wa_ref[...], preferred_element_type=F32)
    mb = jnp.dot(bo_ref[0], wb_ref[...], preferred_element_type=F32)
    m = jax.nn.sigmoid(gates[:, :D_MODEL]) * ma + jax.nn.sigmoid(gates[:, D_MODEL:]) * mb
    x1 = x + mod[2:3] * jnp.dot(m.astype(BF16), wo_ref[...], preferred_element_type=F32)
    h2 = _rms_mod(x1, n2_ref[...], mod[4:5], mod[3:4])
    u = jnp.maximum(jnp.dot(h2.astype(BF16), w1_ref[...], preferred_element_type=F32), 0.0)
    x2 = x1 + mod[5:6] * jnp.dot((u * u).astype(BF16), w2_ref[...], preferred_element_type=F32)
    inv = lax.rsqrt(jnp.mean(x2 * x2, axis=-1, keepdims=True) + RMS_EPS)
    y_ref[0] = (x2 * inv) * fg_ref[...]


def _post_attention(x, a_o, b_o, mod, mod_row, n1, n2, fg, wg, wa, wb, wo, w1, w2, tm):
    b, t, d = x.shape
    tok = lambda w: pl.BlockSpec((1, tm, w), lambda i, j: (i, j, 0))
    weights = [wg, wa, wb, wo, w1, w2]
    return pl.pallas_call(
        _post_kernel,
        grid=(b, t // tm),
        in_specs=[tok(d), tok(A_WIDTH), tok(B_WIDTH),
                  pl.BlockSpec((1, N_MOD, d), lambda i, j: (mod_row(i), 0, 0)),
                  _const_spec((1, d)), _const_spec((1, d)), _const_spec((1, d))]
                 + [_const_spec(w.shape) for w in weights],
        out_specs=tok(d),
        out_shape=jax.ShapeDtypeStruct(x.shape, F32),
        compiler_params=pltpu.CompilerParams(dimension_semantics=("arbitrary", "arbitrary"),
                                             vmem_limit_bytes=VMEM_LIMIT),
        name="post_attention",
    )(x, a_o, b_o, mod, n1, n2, fg, *weights)


def _rope_tables(t):
    pos = jnp.arange(t, dtype=jnp.int32)
    row = (pos // GRID_W).astype(F32)
    col = (pos % GRID_W).astype(F32)
    inv = ROPE_BASE ** (-jnp.arange(ROPE_PAIRS, dtype=F32) / ROPE_PAIRS)
    ar = row[:, None] * inv
    ac = col[:, None] * inv
    cos = jnp.concatenate([jnp.cos(ar), jnp.cos(ar), jnp.cos(ac), jnp.cos(ac)], axis=-1)
    sin = jnp.concatenate([-jnp.sin(ar), jnp.sin(ar), -jnp.sin(ac), jnp.sin(ac)], axis=-1)
    return jnp.tile(cos, (1, LANES // HEAD_DIM)), jnp.tile(sin, (1, LANES // HEAD_DIM))


def _dup_heads(x):
    b, t, kv, dh = x.shape
    return jnp.broadcast_to(x[:, :, :, None, :], (b, t, kv, 2, dh)).reshape(b, t, kv * 2 * dh).astype(BF16)


def kernel(x_prompt, x_sample, cache_a_k, cache_a_v, cache_b_k, cache_b_v, c, c_ctx, w_mod, b_mod,
           norm1_g, norm2_g, w_in, q_norm_g, k_norm_g, nat_bias, w_br_a, w_br_b, w_out, w_mlp_in,
           w_mlp_out, final_norm_g):
    assert w_mod.shape[0] == 1, "single-layer trunk"
    nb, seq, d = x_prompt.shape
    nd, dseq, _ = x_sample.shape
    past = cache_a_k.shape[2]

    ctx_row = nd
    pad = (-(nd + 1)) % 8
    cvec = jnp.concatenate([c, c_ctx[None, :], jnp.zeros((pad, d), F32)], axis=0)
    mod = _modulation(cvec, w_mod[0], b_mod[0]).reshape(cvec.shape[0], N_MOD, d)

    w_qkv = w_in[0, :, :QKV_WIDTH].astype(BF16)
    w_gate = w_in[0, :, QKV_WIDTH:].astype(BF16)
    wa = w_br_a[0].astype(BF16)
    wb = w_br_b[0].astype(BF16)
    wo = w_out[0].astype(BF16)
    w1 = w_mlp_in[0].astype(BF16)
    w2 = w_mlp_out[0].astype(BF16)
    n1 = norm1_g[0].reshape(1, d)
    n2 = norm2_g[0].reshape(1, d)
    fg = final_norm_g.reshape(1, d)
    qg = (jnp.tile(q_norm_g[0], A_HEADS) * Q_SCALE).reshape(1, A_WIDTH)
    kg = jnp.tile(k_norm_g[0], A_KV_HEADS).reshape(1, A_KV_WIDTH)
    head_id = np.arange(LANES) // HEAD_DIM
    hm = jnp.asarray(head_id[:, None] == head_id[None, :], dtype=BF16)

    bias = _expand_bias(nat_bias[0])

    ctx_mod_row = lambda i: ctx_row
    (qa, ka, va, qb, kb, vb, new_ak, new_av, new_bk, new_bv) = _pre_attention(
        x_prompt, mod, ctx_mod_row, n1, w_qkv, qg, kg, hm, None, True, seq)
    a_o = _shared_attention(qa, [(ka, va)], seq, [seq])
    b_o = _full_attention(qb, kb, vb)
    y_prompt = _post_attention(x_prompt, a_o, b_o, mod, ctx_mod_row, n1, n2, fg,
                               w_gate, wa, wb, wo, w1, w2, seq)

    lat_mod_row = lambda i: i
    (qa, ka, va, qb, kb, vb) = _pre_attention(
        x_sample, mod, lat_mod_row, n1, w_qkv, qg, kg, hm, _rope_tables(dseq), False, 512)
    ca_k = _dup_heads(cache_a_k[:, 0])
    ca_vt = jnp.transpose(cache_a_v[:, 0], (0, 2, 3, 1)).astype(BF16)
    cb_k = cache_b_k[:, 0].reshape(nd, past, B_WIDTH).astype(BF16)
    cb_v = cache_b_v[:, 0].reshape(nd, past, B_WIDTH).astype(BF16)
    a_o = _shared_attention(qa, [(ca_k, ca_vt), (ka, va)], 512, [256, 256])
    b_o = _nat_attention(qb, kb, vb, cb_k, cb_v, bias)
    y_sample = _post_attention(x_sample, a_o, b_o, mod, lat_mod_row, n1, n2, fg,
                               w_gate, wa, wb, wo, w1, w2, 256)

    return (y_prompt, y_sample,
            new_ak.reshape(nb, 1, seq, A_KV_HEADS, HEAD_DIM),
            new_av.reshape(nb, 1, seq, A_KV_HEADS, HEAD_DIM),
            new_bk.reshape(nb, 1, seq, B_HEADS, HEAD_DIM),
            new_bv.reshape(nb, 1, seq, B_HEADS, HEAD_DIM))
```

```python
import functools

import numpy as np
import jax
import jax.numpy as jnp
from jax import lax
from jax.experimental import pallas as pl
from jax.experimental.pallas import tpu as pltpu

F32 = jnp.float32
BF16 = jnp.bfloat16

D_MODEL = 1024
GRID_W = 64
HEAD_DIM = 64
A_HEADS = 8
A_KV_HEADS = 2
B_HEADS = 8
A_WIDTH = A_HEADS * HEAD_DIM
A_KV_WIDTH = A_KV_HEADS * HEAD_DIM
B_WIDTH = B_HEADS * HEAD_DIM
D_FF = 4 * D_MODEL
WIN_H = 8
WIN_W = 16
ROPE_BASE = 10000.0
ROPE_PAIRS = HEAD_DIM // 4
RMS_EPS = 1e-6
N_MOD = 6
NEG_INF = -1e30

LANES = 128
PAIRS = A_WIDTH // LANES
QKV_WIDTH = A_WIDTH + 2 * A_KV_WIDTH + 3 * B_WIDTH
GATE_WIDTH = 2 * D_MODEL
NAT_Q_ROWS = 4
NAT_K_ROWS = 12
PRE_SUB_ROWS = 256
VMEM_LIMIT = 56 * 1024 * 1024
LOG2E = 1.4426950408889634
Q_SCALE = HEAD_DIM ** -0.5 * LOG2E


def _const_spec(shape):
    nd = len(shape)
    return pl.BlockSpec(shape, lambda *_: (0,) * nd, pipeline_mode=pl.Buffered(1))


def _rms_mod(x, g, scale, shift):
    inv = lax.rsqrt(jnp.mean(x * x, axis=-1, keepdims=True) + RMS_EPS)
    return (x * inv) * g * (1.0 + scale) + shift


def _mod_kernel(c_ref, w_ref, b_ref, o_ref):
    cv = c_ref[...]
    s = cv * jax.nn.sigmoid(cv)
    o_ref[...] = jnp.dot(s.astype(BF16), w_ref[...].astype(BF16),
                         preferred_element_type=F32) + b_ref[...]


def _modulation(cvec, w_mod, b_mod):
    n, d = cvec.shape
    width = w_mod.shape[1]
    tn = 1024
    return pl.pallas_call(
        _mod_kernel,
        grid=(width // tn,),
        in_specs=[pl.BlockSpec((n, d), lambda j: (0, 0)),
                  pl.BlockSpec((d, tn), lambda j: (0, j)),
                  pl.BlockSpec((1, tn), lambda j: (0, j))],
        out_specs=pl.BlockSpec((n, tn), lambda j: (0, j)),
        out_shape=jax.ShapeDtypeStruct((n, width), F32),
        compiler_params=pltpu.CompilerParams(dimension_semantics=("arbitrary",),
                                             vmem_limit_bytes=VMEM_LIMIT),
        name="modulation",
    )(cvec, w_mod, b_mod.reshape(1, width))


def _nat_variant_geometry(variant):
    if variant == 0:
        return 0, (lambda i, n: n < WIN_H)
    if variant == 1:
        return -(WIN_H // 2), (lambda i, n: 0 <= n - i < WIN_H)
    return -WIN_H, (lambda i, n: n >= NAT_K_ROWS - WIN_H)


def _bias_kernel(r_ref, o_ref):
    h = pl.program_id(0)
    n_dr = 2 * WIN_H - 1
    n_dc = 2 * WIN_W - 1
    kc = lax.broadcasted_iota(jnp.int32, (GRID_W, LANES), 0)
    lane = lax.broadcasted_iota(jnp.int32, (GRID_W, LANES), 1)
    qc = lane & (GRID_W - 1)
    hi_half = lane >= GRID_W
    dc = kc - qc + (WIN_W - 1)
    cstart = jnp.clip(qc - WIN_W // 2, 0, GRID_W - WIN_W)
    col_ok = (kc >= cstart) & (kc < cstart + WIN_W)
    neg = jnp.full((GRID_W, LANES), NEG_INF, F32)
    base = h * (n_dr * n_dc)
    tiles = []
    for a in range(n_dr):
        val = neg
        for b in range(n_dc):
            val = jnp.where(dc == b, r_ref[base + a * n_dc + b] * LOG2E, val)
        tiles.append(jnp.where(col_ok, val, neg))
    for variant in range(3):
        off, valid = _nat_variant_geometry(variant)
        for n in range(NAT_K_ROWS):
            for p in range(NAT_Q_ROWS // 2):
                halves = []
                for i in (2 * p, 2 * p + 1):
                    a = n - i + off + WIN_H - 1
                    halves.append(tiles[a] if valid(i, n) else neg)
                o_ref[variant, 0, n * GRID_W:(n + 1) * GRID_W, p * LANES:(p + 1) * LANES] = (
                    jnp.where(hi_half, halves[1], halves[0]))


def _expand_bias(nat_bias):
    heads = nat_bias.shape[0]
    nq = NAT_Q_ROWS * GRID_W
    nk = NAT_K_ROWS * GRID_W
    return pl.pallas_call(
        _bias_kernel,
        grid=(heads,),
        in_specs=[pl.BlockSpec(memory_space=pltpu.SMEM)],
        out_specs=pl.BlockSpec((3, 1, nk, nq), lambda h: (0, h, 0, 0)),
        out_shape=jax.ShapeDtypeStruct((3, heads, nk, nq), F32),
        compiler_params=pltpu.CompilerParams(dimension_semantics=("arbitrary",),
                                             vmem_limit_bytes=VMEM_LIMIT),
        name="bias_expand",
    )(nat_bias.reshape(-1))


def _pre_kernel(*refs, rope, emit_cache):
    x_ref, mod_ref, n1_ref, w_ref, qg_ref, kg_ref, hm_ref = refs[:7]
    pos = 7
    if rope:
        cos_ref, sin_ref = refs[pos:pos + 2]
        pos += 2
    qa_ref, ka_ref, va_ref, qb_ref, kb_ref, vb_ref = refs[pos:pos + 6]
    pos += 6
    if emit_cache:
        cak_ref, cav_ref, cbk_ref, cbv_ref = refs[pos:pos + 4]

    mod = mod_ref[0]
    tm = x_ref.shape[1]
    sub = min(tm, PRE_SUB_ROWS)
    row_slices = [slice(r, r + sub) for r in range(0, tm, sub)]
    hs = [_rms_mod(x_ref[0, rows], n1_ref[...], mod[1:2], mod[0:1]).astype(BF16) for rows in row_slices]
    accs = [jnp.dot(h, w_ref[...], preferred_element_type=F32) for h in hs]
    lane = lax.broadcasted_iota(jnp.int32, (sub, LANES), 1)
    lo_half = lane < HEAD_DIM
    hm = hm_ref[...]

    def head_norm(y, g):
        ms = jnp.dot((y * y).astype(BF16), hm, preferred_element_type=F32) * (1.0 / HEAD_DIM)
        return (y * lax.rsqrt(ms + RMS_EPS)) * g

    for rows, acc in zip(row_slices, accs):
        def rotary(y):
            partner = jnp.where((lane & ROPE_PAIRS) == 0,
                                pltpu.roll(y, LANES - ROPE_PAIRS, 1), pltpu.roll(y, ROPE_PAIRS, 1))
            return y * cos_ref[rows] + partner * sin_ref[rows]

        def store_dup(ref, y):
            r = pltpu.roll(y, HEAD_DIM, 1)
            ref[0, rows, 0:LANES] = jnp.where(lo_half, y, r).astype(ref.dtype)
            ref[0, rows, LANES:2 * LANES] = jnp.where(lo_half, r, y).astype(ref.dtype)

        for c in range(PAIRS):
            sl = slice(c * LANES, (c + 1) * LANES)
            y = head_norm(acc[:, sl], qg_ref[:, sl])
            if rope:
                y = rotary(y)
            qa_ref[0, c, :, rows] = y.T.astype(qa_ref.dtype)

        k = head_norm(acc[:, A_WIDTH:A_WIDTH + A_KV_WIDTH], kg_ref[...])
        v = acc[:, A_WIDTH + A_KV_WIDTH:A_WIDTH + 2 * A_KV_WIDTH]
        if emit_cache:
            cak_ref[0, rows] = k
            cav_ref[0, rows] = v
        if rope:
            k = rotary(k)
        store_dup(ka_ref, k)
        vt = v.T
        for g in range(A_KV_HEADS):
            va_ref[0, g, :, rows] = vt[g * HEAD_DIM:(g + 1) * HEAD_DIM].astype(va_ref.dtype)

        b0 = A_WIDTH + 2 * A_KV_WIDTH
        bq = acc[:, b0:b0 + B_WIDTH]
        bk = acc[:, b0 + B_WIDTH:b0 + 2 * B_WIDTH]
        bv = acc[:, b0 + 2 * B_WIDTH:b0 + 3 * B_WIDTH]
        for c in range(PAIRS):
            qb_ref[0, c, :, rows] = (bq[:, c * LANES:(c + 1) * LANES] * Q_SCALE).T.astype(qb_ref.dtype)
        kb_ref[0, rows] = bk.astype(kb_ref.dtype)
        vb_ref[0, rows] = bv.astype(vb_ref.dtype)
        if emit_cache:
            cbk_ref[0, rows] = bk
            cbv_ref[0, rows] = bv


def _pre_attention(x, mod, mod_row, n1, w_qkv, qg, kg, hm, rope_tables, emit_cache, tm):
    b, t, d = x.shape
    rope = rope_tables is not None
    tok = lambda w: pl.BlockSpec((1, tm, w), lambda i, j: (i, j, 0))
    in_specs = [tok(d),
                pl.BlockSpec((1, N_MOD, d), lambda i, j: (mod_row(i), 0, 0)),
                _const_spec((1, d)), _const_spec(w_qkv.shape), _const_spec(qg.shape),
                _const_spec(kg.shape), _const_spec(hm.shape)]
    args = [x, mod, n1, w_qkv, qg, kg, hm]
    if rope:
        in_specs += [pl.BlockSpec((tm, LANES), lambda i, j: (j, 0))] * 2
        args += list(rope_tables)
    qt = (PAIRS, LANES)
    vt = (A_KV_HEADS, HEAD_DIM)
    layouts = [qt, 2 * LANES, vt, qt, B_WIDTH, B_WIDTH]
    out_specs = [tok(w) if isinstance(w, int) else
                 pl.BlockSpec((1,) + w + (tm,), lambda i, j: (i, 0, 0, j)) for w in layouts]
    out_shape = [jax.ShapeDtypeStruct((b, t, w) if isinstance(w, int) else (b,) + w + (t,), BF16)
                 for w in layouts]
    if emit_cache:
        cw = [A_KV_WIDTH, A_KV_WIDTH, B_WIDTH, B_WIDTH]
        out_specs += [tok(w) for w in cw]
        out_shape += [jax.ShapeDtypeStruct((b, t, w), F32) for w in cw]
    return pl.pallas_call(
        functools.partial(_pre_kernel, rope=rope, emit_cache=emit_cache),
        grid=(b, t // tm),
        in_specs=in_specs, out_specs=out_specs, out_shape=out_shape,
        compiler_params=pltpu.CompilerParams(dimension_semantics=("arbitrary", "arbitrary"),
                                             vmem_limit_bytes=VMEM_LIMIT),
        name="pre_attention",
    )(*args)


ONES_ROWS = 16


def _flash_t(streams, kv_sets, online, bias=None):
    items = [(si, start, chunk, st)
             for si, (n_keys, chunk, _, _) in enumerate(kv_sets)
             for start in range(0, n_keys, chunk)
             for st in range(len(streams))]

    def scores(item):
        si, start, size, st = item
        k = kv_sets[si][2](st, start, size)
        s = jnp.dot(k, streams[st][0], preferred_element_type=F32)
        if bias is not None and si == bias[0]:
            s = s + bias[1](st)
        return s

    carries = [None] * len(streams)
    pending = None
    for idx, item in enumerate(items):
        s = pending if pending is not None else scores(item)
        pending = None
        si, start, size, st = item
        if (online or carries[st] is None) and idx + 1 < len(items):
            pending = scores(items[idx + 1])
        ones = jnp.ones((ONES_ROWS, size), BF16)
        v_augs = [jnp.concatenate([vt, ones], axis=0) for vt in kv_sets[si][3](st, start, size)]
        alpha = None
        if carries[st] is None:
            shift, prev = jnp.max(s, axis=0, keepdims=True), None
        elif online:
            m_prev, prev = carries[st]
            shift = jnp.maximum(m_prev, jnp.max(s, axis=0, keepdims=True))
            alpha = jnp.exp2(m_prev - shift)
        else:
            shift, prev = carries[st]
        p = jnp.exp2(s - shift).astype(BF16)
        accs = []
        for gi, (c0, cn) in enumerate(streams[st][1]):
            acc = jnp.dot(v_augs[gi], p[:, c0:c0 + cn], preferred_element_type=F32)
            if prev is not None:
                acc = (prev[gi] if alpha is None else prev[gi] * alpha[:, c0:c0 + cn]) + acc
            accs.append(acc)
        carries[st] = (shift, accs)

    outs = [[acc[:HEAD_DIM] / acc[HEAD_DIM:HEAD_DIM + 1] for acc in accs] for _, accs in carries]
    flags = []
    for _, accs in carries:
        for acc in accs:
            bad = jnp.max(jnp.where(jnp.isfinite(acc), 0.0, 1.0), axis=0, keepdims=True)
            flags += [bad[:, c:c + LANES] for c in range(0, bad.shape[1], LANES)]
    return outs, functools.reduce(jnp.maximum, flags)


def _masked_heads(qt):
    lo_rows = lax.broadcasted_iota(jnp.int32, qt.shape, 0) < HEAD_DIM
    zero = jnp.zeros_like(qt)
    return jnp.where(lo_rows, qt, zero), jnp.where(lo_rows, zero, qt)


def _split_vt(v):
    vt = v.T
    return [vt[:HEAD_DIM], vt[HEAD_DIM:]]


EXCESS_TILE = (8, LANES)


def _store_tiles(o_ref, ex_ref, tiles, overflow):
    for p, o in enumerate(tiles):
        o_ref[0, :, p * LANES:(p + 1) * LANES] = o.astype(o_ref.dtype)
    if ex_ref is not None:
        ex_ref[0, 0, 0] = jnp.broadcast_to(overflow, EXCESS_TILE)


def _guarded(call):
    out, overflow = call(False)
    return lax.cond(jnp.max(overflow) == 0.0, lambda: out, lambda: call(True)[0])


def _attn_shared_kernel(q_ref, *refs, chunks, online, guard):
    o_ref, ex_ref = (refs[-2], refs[-1]) if guard else (refs[-1], None)
    kv = refs[:2 * len(chunks)]
    npairs, tq = q_ref.shape[1], q_ref.shape[3]
    kv_sets = [(kv[2 * i].shape[1], chunk,
                lambda st, start, size, r=kv[2 * i]: r[0, start:start + size, :],
                lambda st, start, size, r=kv[2 * i + 1]: [r[0, 0, :, start:start + size]])
               for i, chunk in enumerate(chunks)]
    masked = [_masked_heads(q_ref[0, p]) for p in range(npairs)]
    q_t = jnp.concatenate([m[0] for m in masked] + [m[1] for m in masked], axis=1)
    outs, overflow = _flash_t([(q_t, [(0, 2 * npairs * tq)])], kv_sets, online)
    o_t = outs[0][0]
    tiles = [jnp.concatenate([o_t[:, p * tq:(p + 1) * tq],
                              o_t[:, (npairs + p) * tq:(npairs + p + 1) * tq]], axis=0).T
             for p in range(npairs)]
    _store_tiles(o_ref, ex_ref, tiles, overflow)


def _attention_outputs(out_spec, out_sds, grid, guard):
    if not guard:
        return out_spec, out_sds
    ex_spec = pl.BlockSpec((1, 1, 1) + EXCESS_TILE, lambda i, c, j: (i, c, j, 0, 0))
    return [out_spec, ex_spec], [out_sds, jax.ShapeDtypeStruct(tuple(grid) + EXCESS_TILE, F32)]


def _shared_attention(qt, kv_sets, tq, chunks):
    b, _, _, t = qt.shape
    pairs_per_step = PAIRS // A_KV_HEADS
    in_specs = [pl.BlockSpec((1, pairs_per_step, LANES, tq), lambda i, c, j: (i, c, 0, j))]
    args = [qt]
    for k, vt in kv_sets:
        nk = k.shape[1]
        in_specs += [pl.BlockSpec((1, nk, LANES), lambda i, c, j: (i, 0, c)),
                     pl.BlockSpec((1, 1, HEAD_DIM, nk), lambda i, c, j: (i, c, 0, 0))]
        args += [k, vt]
    grid = (b, A_KV_HEADS, t // tq)
    guard = sum(k.shape[1] // chunk for (k, _), chunk in zip(kv_sets, chunks)) > 1
    out_specs, out_shape = _attention_outputs(
        pl.BlockSpec((1, tq, pairs_per_step * LANES), lambda i, c, j: (i, j, c)),
        jax.ShapeDtypeStruct((b, t, A_WIDTH), BF16), grid, guard)

    def call(online):
        return pl.pallas_call(
            functools.partial(_attn_shared_kernel, chunks=tuple(chunks), online=online, guard=guard),
            grid=grid, in_specs=in_specs, out_specs=out_specs, out_shape=out_shape,
            compiler_params=pltpu.CompilerParams(dimension_semantics=("arbitrary",) * 3,
                                                 vmem_limit_bytes=VMEM_LIMIT),
            name="shared_attention_online" if online else "shared_attention",
        )(*args)

    return _guarded(call) if guard else call(False)


def _pair_streams(q_ref):
    tq = q_ref.shape[3]
    return [(jnp.concatenate(_masked_heads(q_ref[0, p]), axis=1), [(0, tq), (tq, tq)])
            for p in range(q_ref.shape[1])]


def _pair_tiles(outs):
    return [jnp.concatenate(groups, axis=0).T for groups in outs]


def _pair_loaders(k_ref, v_ref, rows):
    lanes = lambda st: slice(st * LANES, (st + 1) * LANES)
    return (lambda st, start, size: k_ref[0, rows(start, size), lanes(st)],
            lambda st, start, size: _split_vt(v_ref[0, rows(start, size), lanes(st)]))


def _attn_full_kernel(q_ref, k_ref, v_ref, o_ref):
    nk = k_ref.shape[1]
    kv_sets = [(nk, nk) + _pair_loaders(k_ref, v_ref, lambda start, size: slice(start, start + size))]
    outs, overflow = _flash_t(_pair_streams(q_ref), kv_sets, False)
    _store_tiles(o_ref, None, _pair_tiles(outs), overflow)


def _full_attention(qt, k, v):
    b, npairs, _, t = qt.shape
    tok = pl.BlockSpec((1, t, npairs * LANES), lambda i: (i, 0, 0))
    return pl.pallas_call(
        _attn_full_kernel,
        grid=(b,),
        in_specs=[pl.BlockSpec((1, npairs, LANES, t), lambda i: (i, 0, 0, 0)), tok, tok],
        out_specs=tok,
        out_shape=jax.ShapeDtypeStruct((b, t, npairs * LANES), BF16),
        compiler_params=pltpu.CompilerParams(dimension_semantics=("arbitrary",),
                                             vmem_limit_bytes=VMEM_LIMIT),
        name="full_attention",
    )(qt, k, v)


NAT_PAIRS_PER_STEP = 4


def _attn_nat_kernel(q_ref, kw_ref, vw_ref, kc_ref, vc_ref, bias_ref, o_ref, ex_ref, *, rows, online):
    j = pl.program_id(2)
    ws = jnp.clip(j * NAT_Q_ROWS - WIN_H // 2, 0, rows - NAT_K_ROWS)
    win = pl.multiple_of(ws * GRID_W, GRID_W)
    nk = NAT_K_ROWS * GRID_W
    nc = kc_ref.shape[1]
    kv_sets = [(nc, nc) + _pair_loaders(kc_ref, vc_ref, lambda start, size: slice(start, start + size)),
               (nk, nk) + _pair_loaders(kw_ref, vw_ref, lambda start, size: pl.ds(win, nk))]
    load_bias = lambda st: jnp.concatenate([bias_ref[0, 2 * st], bias_ref[0, 2 * st + 1]], axis=1)

    outs, overflow = _flash_t(_pair_streams(q_ref), kv_sets, online, bias=(1, load_bias))
    _store_tiles(o_ref, ex_ref, _pair_tiles(outs), overflow)


def _nat_attention(qt, k, v, ctx_k, ctx_v, bias):
    b, _, _, t = qt.shape
    rows = t // GRID_W
    nblk = rows // NAT_Q_ROWS
    tq = NAT_Q_ROWS * GRID_W
    nk = NAT_K_ROWS * GRID_W
    width = NAT_PAIRS_PER_STEP * LANES
    whole = lambda n: pl.BlockSpec((1, n, width), lambda i, c, j: (i, 0, c))

    def bias_map(i, c, j):
        variant = jnp.where(j == 0, 0, jnp.where(j == nblk - 1, 2, 1))
        return (variant, c, 0, 0)

    grid = (b, PAIRS // NAT_PAIRS_PER_STEP, nblk)
    out_specs, out_shape = _attention_outputs(
        pl.BlockSpec((1, tq, width), lambda i, c, j: (i, j, c)),
        jax.ShapeDtypeStruct((b, t, B_WIDTH), BF16), grid, True)

    def call(online):
        return pl.pallas_call(
            functools.partial(_attn_nat_kernel, rows=rows, online=online),
            grid=grid,
            in_specs=[pl.BlockSpec((1, NAT_PAIRS_PER_STEP, LANES, tq), lambda i, c, j: (i, c, 0, j)),
                      whole(t), whole(t), whole(ctx_k.shape[1]), whole(ctx_v.shape[1]),
                      pl.BlockSpec((1, 2 * NAT_PAIRS_PER_STEP, nk, tq), bias_map)],
            out_specs=out_specs, out_shape=out_shape,
            compiler_params=pltpu.CompilerParams(dimension_semantics=("arbitrary",) * 3,
                                                 vmem_limit_bytes=VMEM_LIMIT),
            name="nat_attention_online" if online else "nat_attention",
        )(qt, k, v, ctx_k, ctx_v, bias)

    return _guarded(call)


def _post_kernel(x_ref, ao_ref, bo_ref, mod_ref, n1_ref, n2_ref, fg_ref,
                 wg_ref, wa_ref, wb_ref, wo_ref, w1_ref, w2_ref, y_ref):
    x = x_ref[0]
    mod = mod_ref[0]
    h = _rms_mod(x, n1_ref[...], mod[1:2], mod[0:1])
    gates = jnp.dot(h.astype(BF16), wg_ref[...], preferred_element_type=F32)
    ma = jnp.dot(ao_ref[0], wa_ref[...], preferred_element_type=F32)
    mb = jnp.dot(bo_ref[0], wb_ref[...], preferred_element_type=F32)
    m = jax.nn.sigmoid(gates[:, :D_MODEL]) * ma + jax.nn.sigmoid(gates[:, D_MODEL:]) * mb
    x1 = x + mod[2:3] * jnp.dot(m.astype(BF16), wo_ref[...], preferred_element_type=F32)
    h2 = _rms_mod(x1, n2_ref[...], mod[4:5], mod[3:4])
    u = jnp.maximum(jnp.dot(h2.astype(BF16), w1_ref[...], preferred_element_type=F32), 0.0)
    x2 = x1 + mod[5:6] * jnp.dot((u * u).astype(BF16), w2_ref[...], preferred_element_type=F32)
    inv = lax.rsqrt(jnp.mean(x2 * x2, axis=-1, keepdims=True) + RMS_EPS)
    y_ref[0] = (x2 * inv) * fg_ref[...]


def _post_attention(x, a_o, b_o, mod, mod_row, n1, n2, fg, wg, wa, wb, wo, w1, w2, tm):
    b, t, d = x.shape
    tok = lambda w: pl.BlockSpec((1, tm, w), lambda i, j: (i, j, 0))
    weights = [wg, wa, wb, wo, w1, w2]
    return pl.pallas_call(
        _post_kernel,
        grid=(b, t // tm),
        in_specs=[tok(d), tok(A_WIDTH), tok(B_WIDTH),
                  pl.BlockSpec((1, N_MOD, d), lambda i, j: (mod_row(i), 0, 0)),
                  _const_spec((1, d)), _const_spec((1, d)), _const_spec((1, d))]
                 + [_const_spec(w.shape) for w in weights],
        out_specs=tok(d),
        out_shape=jax.ShapeDtypeStruct(x.shape, F32),
        compiler_params=pltpu.CompilerParams(dimension_semantics=("arbitrary", "arbitrary"),
                                             vmem_limit_bytes=VMEM_LIMIT),
        name="post_attention",
    )(x, a_o, b_o, mod, n1, n2, fg, *weights)


def _rope_tables(t):
    pos = jnp.arange(t, dtype=jnp.int32)
    row = (pos // GRID_W).astype(F32)
    col = (pos % GRID_W).astype(F32)
    inv = ROPE_BASE ** (-jnp.arange(ROPE_PAIRS, dtype=F32) / ROPE_PAIRS)
    ar = row[:, None] * inv
    ac = col[:, None] * inv
    cos = jnp.concatenate([jnp.cos(ar), jnp.cos(ar), jnp.cos(ac), jnp.cos(ac)], axis=-1)
    sin = jnp.concatenate([-jnp.sin(ar), jnp.sin(ar), -jnp.sin(ac), jnp.sin(ac)], axis=-1)
    return jnp.tile(cos, (1, LANES // HEAD_DIM)), jnp.tile(sin, (1, LANES // HEAD_DIM))


def _dup_heads(x):
    b, t, kv, dh = x.shape
    return jnp.broadcast_to(x[:, :, :, None, :], (b, t, kv, 2, dh)).reshape(b, t, kv * 2 * dh).astype(BF16)


def kernel(x_prompt, x_sample, cache_a_k, cache_a_v, cache_b_k, cache_b_v, c, c_ctx, w_mod, b_mod,
           norm1_g, norm2_g, w_in, q_norm_g, k_norm_g, nat_bias, w_br_a, w_br_b, w_out, w_mlp_in,
           w_mlp_out, final_norm_g):
    assert w_mod.shape[0] == 1, "single-layer trunk"
    nb, seq, d = x_prompt.shape
    nd, dseq, _ = x_sample.shape
    past = cache_a_k.shape[2]

    ctx_row = nd
    pad = (-(nd + 1)) % 8
    cvec = jnp.concatenate([c, c_ctx[None, :], jnp.zeros((pad, d), F32)], axis=0)
    mod = _modulation(cvec, w_mod[0], b_mod[0]).reshape(cvec.shape[0], N_MOD, d)

    w_qkv = w_in[0, :, :QKV_WIDTH].astype(BF16)
    w_gate = w_in[0, :, QKV_WIDTH:].astype(BF16)
    wa = w_br_a[0].astype(BF16)
    wb = w_br_b[0].astype(BF16)
    wo = w_out[0].astype(BF16)
    w1 = w_mlp_in[0].astype(BF16)
    w2 = w_mlp_out[0].astype(BF16)
    n1 = norm1_g[0].reshape(1, d)
    n2 = norm2_g[0].reshape(1, d)
    fg = final_norm_g.reshape(1, d)
    qg = (jnp.tile(q_norm_g[0], A_HEADS) * Q_SCALE).reshape(1, A_WIDTH)
    kg = jnp.tile(k_norm_g[0], A_KV_HEADS).reshape(1, A_KV_WIDTH)
    head_id = np.arange(LANES) // HEAD_DIM
    hm = jnp.asarray(head_id[:, None] == head_id[None, :], dtype=BF16)

    bias = _expand_bias(nat_bias[0])

    ctx_mod_row = lambda i: ctx_row
    (qa, ka, va, qb, kb, vb, new_ak, new_av, new_bk, new_bv) = _pre_attention(
        x_prompt, mod, ctx_mod_row, n1, w_qkv, qg, kg, hm, None, True, seq)
    a_o = _shared_attention(qa, [(ka, va)], seq, [seq])
    b_o = _full_attention(qb, kb, vb)
    y_prompt = _post_attention(x_prompt, a_o, b_o, mod, ctx_mod_row, n1, n2, fg,
                               w_gate, wa, wb, wo, w1, w2, seq)

    lat_mod_row = lambda i: i
    (qa, ka, va, qb, kb, vb) = _pre_attention(
        x_sample, mod, lat_mod_row, n1, w_qkv, qg, kg, hm, _rope_tables(dseq), False, 512)
    ca_k = _dup_heads(cache_a_k[:, 0])
    ca_vt = jnp.transpose(cache_a_v[:, 0], (0, 2, 3, 1)).astype(BF16)
    cb_k = cache_b_k[:, 0].reshape(nd, past, B_WIDTH).astype(BF16)
    cb_v = cache_b_v[:, 0].reshape(nd, past, B_WIDTH).astype(BF16)
    a_o = _shared_attention(qa, [(ca_k, ca_vt), (ka, va)], 1024, [256, 256])
    b_o = _nat_attention(qb, kb, vb, cb_k, cb_v, bias)
    y_sample = _post_attention(x_sample, a_o, b_o, mod, lat_mod_row, n1, n2, fg,
                               w_gate, wa, wb, wo, w1, w2, 512)

    return (y_prompt, y_sample,
            new_ak.reshape(nb, 1, seq, A_KV_HEADS, HEAD_DIM),
            new_av.reshape(nb, 1, seq, A_KV_HEADS, HEAD_DIM),
            new_bk.reshape(nb, 1, seq, B_HEADS, HEAD_DIM),
            new_bv.reshape(nb, 1, seq, B_HEADS, HEAD_DIM))
```

```python
import functools

import numpy as np
import jax
import jax.numpy as jnp
from jax import lax
from jax.experimental import pallas as pl
from jax.experimental.pallas import tpu as pltpu

F32 = jnp.float32
BF16 = jnp.bfloat16

D_MODEL = 1024
GRID_W = 64
HEAD_DIM = 64
A_HEADS = 8
A_KV_HEADS = 2
B_HEADS = 8
A_WIDTH = A_HEADS * HEAD_DIM
A_KV_WIDTH = A_KV_HEADS * HEAD_DIM
B_WIDTH = B_HEADS * HEAD_DIM
D_FF = 4 * D_MODEL
WIN_H = 8
WIN_W = 16
ROPE_BASE = 10000.0
ROPE_PAIRS = HEAD_DIM // 4
RMS_EPS = 1e-6
N_MOD = 6
NEG_INF = -1e30

LANES = 128
PAIRS = A_WIDTH // LANES
QKV_WIDTH = A_WIDTH + 2 * A_KV_WIDTH + 3 * B_WIDTH
GATE_WIDTH = 2 * D_MODEL
NAT_Q_ROWS = 4
NAT_K_ROWS = 12
PRE_SUB_ROWS = 256
VMEM_LIMIT = 56 * 1024 * 1024
LOG2E = 1.4426950408889634
Q_SCALE = HEAD_DIM ** -0.5 * LOG2E


def _const_spec(shape):
    nd = len(shape)
    return pl.BlockSpec(shape, lambda *_: (0,) * nd, pipeline_mode=pl.Buffered(1))


def _rms_mod(x, g, scale, shift):
    inv = lax.rsqrt(jnp.mean(x * x, axis=-1, keepdims=True) + RMS_EPS)
    return (x * inv) * g * (1.0 + scale) + shift


def _mod_kernel(c_ref, w_ref, b_ref, o_ref):
    cv = c_ref[...]
    s = cv * jax.nn.sigmoid(cv)
    o_ref[...] = jnp.dot(s.astype(BF16), w_ref[...].astype(BF16),
                         preferred_element_type=F32) + b_ref[...]


def _modulation(cvec, w_mod, b_mod):
    n, d = cvec.shape
    width = w_mod.shape[1]
    tn = 1024
    return pl.pallas_call(
        _mod_kernel,
        grid=(width // tn,),
        in_specs=[pl.BlockSpec((n, d), lambda j: (0, 0)),
                  pl.BlockSpec((d, tn), lambda j: (0, j)),
                  pl.BlockSpec((1, tn), lambda j: (0, j))],
        out_specs=pl.BlockSpec((n, tn), lambda j: (0, j)),
        out_shape=jax.ShapeDtypeStruct((n, width), F32),
        compiler_params=pltpu.CompilerParams(dimension_semantics=("arbitrary",),
                                             vmem_limit_bytes=VMEM_LIMIT),
        name="modulation",
    )(cvec, w_mod, b_mod.reshape(1, width))


def _nat_variant_geometry(variant):
    if variant == 0:
        return 0, (lambda i, n: n < WIN_H)
    if variant == 1:
        return -(WIN_H // 2), (lambda i, n: 0 <= n - i < WIN_H)
    return -WIN_H, (lambda i, n: n >= NAT_K_ROWS - WIN_H)


def _bias_kernel(r_ref, o_ref):
    h = pl.program_id(0)
    n_dr = 2 * WIN_H - 1
    n_dc = 2 * WIN_W - 1
    kc = lax.broadcasted_iota(jnp.int32, (GRID_W, LANES), 0)
    lane = lax.broadcasted_iota(jnp.int32, (GRID_W, LANES), 1)
    qc = lane & (GRID_W - 1)
    hi_half = lane >= GRID_W
    dc = kc - qc + (WIN_W - 1)
    cstart = jnp.clip(qc - WIN_W // 2, 0, GRID_W - WIN_W)
    col_ok = (kc >= cstart) & (kc < cstart + WIN_W)
    neg = jnp.full((GRID_W, LANES), NEG_INF, F32)
    base = h * (n_dr * n_dc)
    tiles = []
    for a in range(n_dr):
        val = neg
        for b in range(n_dc):
            val = jnp.where(dc == b, r_ref[base + a * n_dc + b] * LOG2E, val)
        tiles.append(jnp.where(col_ok, val, neg))
    for variant in range(3):
        off, valid = _nat_variant_geometry(variant)
        for n in range(NAT_K_ROWS):
            for p in range(NAT_Q_ROWS // 2):
                halves = []
                for i in (2 * p, 2 * p + 1):
                    a = n - i + off + WIN_H - 1
                    halves.append(tiles[a] if valid(i, n) else neg)
                o_ref[variant, 0, n * GRID_W:(n + 1) * GRID_W, p * LANES:(p + 1) * LANES] = (
                    jnp.where(hi_half, halves[1], halves[0]))


def _expand_bias(nat_bias):
    heads = nat_bias.shape[0]
    nq = NAT_Q_ROWS * GRID_W
    nk = NAT_K_ROWS * GRID_W
    return pl.pallas_call(
        _bias_kernel,
        grid=(heads,),
        in_specs=[pl.BlockSpec(memory_space=pltpu.SMEM)],
        out_specs=pl.BlockSpec((3, 1, nk, nq), lambda h: (0, h, 0, 0)),
        out_shape=jax.ShapeDtypeStruct((3, heads, nk, nq), F32),
        compiler_params=pltpu.CompilerParams(dimension_semantics=("arbitrary",),
                                             vmem_limit_bytes=VMEM_LIMIT),
        name="bias_expand",
    )(nat_bias.reshape(-1))


def _pre_kernel(*refs, rope, emit_cache):
    x_ref, mod_ref, n1_ref, w_ref, qg_ref, kg_ref, hm_ref = refs[:7]
    pos = 7
    if rope:
        cos_ref, sin_ref = refs[pos:pos + 2]
        pos += 2
    qa_ref, ka_ref, va_ref, qb_ref, kb_ref, vb_ref = refs[pos:pos + 6]
    pos += 6
    if emit_cache:
        cak_ref, cav_ref, cbk_ref, cbv_ref = refs[pos:pos + 4]

    mod = mod_ref[0]
    tm = x_ref.shape[1]
    sub = min(tm, PRE_SUB_ROWS)
    row_slices = [slice(r, r + sub) for r in range(0, tm, sub)]
    hs = [_rms_mod(x_ref[0, rows], n1_ref[...], mod[1:2], mod[0:1]).astype(BF16) for rows in row_slices]
    lane = lax.broadcasted_iota(jnp.int32, (sub, LANES), 1)
    lo_half = lane < HEAD_DIM
    hm = hm_ref[...]

    def head_norm(y, g):
        ms = jnp.dot((y * y).astype(BF16), hm, preferred_element_type=F32) * (1.0 / HEAD_DIM)
        return (y * lax.rsqrt(ms + RMS_EPS)) * g

    def epilogue(rows, acc):
        def rotary(y):
            partner = jnp.where((lane & ROPE_PAIRS) == 0,
                                pltpu.roll(y, LANES - ROPE_PAIRS, 1), pltpu.roll(y, ROPE_PAIRS, 1))
            return y * cos_ref[rows] + partner * sin_ref[rows]

        def store_dup(ref, y):
            r = pltpu.roll(y, HEAD_DIM, 1)
            ref[0, rows, 0:LANES] = jnp.where(lo_half, y, r).astype(ref.dtype)
            ref[0, rows, LANES:2 * LANES] = jnp.where(lo_half, r, y).astype(ref.dtype)

        for c in range(PAIRS):
            sl = slice(c * LANES, (c + 1) * LANES)
            y = head_norm(acc[:, sl], qg_ref[:, sl])
            if rope:
                y = rotary(y)
            qa_ref[0, c, :, rows] = y.T.astype(qa_ref.dtype)

        k = head_norm(acc[:, A_WIDTH:A_WIDTH + A_KV_WIDTH], kg_ref[...])
        v = acc[:, A_WIDTH + A_KV_WIDTH:A_WIDTH + 2 * A_KV_WIDTH]
        if emit_cache:
            cak_ref[0, rows] = k
            cav_ref[0, rows] = v
        if rope:
            k = rotary(k)
        store_dup(ka_ref, k)
        vt = v.T
        for g in range(A_KV_HEADS):
            va_ref[0, g, :, rows] = vt[g * HEAD_DIM:(g + 1) * HEAD_DIM].astype(va_ref.dtype)

        b0 = A_WIDTH + 2 * A_KV_WIDTH
        bq = acc[:, b0:b0 + B_WIDTH]
        bk = acc[:, b0 + B_WIDTH:b0 + 2 * B_WIDTH]
        bv = acc[:, b0 + 2 * B_WIDTH:b0 + 3 * B_WIDTH]
        for c in range(PAIRS):
            qb_ref[0, c, :, rows] = (bq[:, c * LANES:(c + 1) * LANES] * Q_SCALE).T.astype(qb_ref.dtype)
        kb_ref[0, rows] = bk.astype(kb_ref.dtype)
        vb_ref[0, rows] = bv.astype(vb_ref.dtype)
        if emit_cache:
            cbk_ref[0, rows] = bk
            cbv_ref[0, rows] = bv

    accs = []
    for i, h in enumerate(hs):
        accs.append(jnp.dot(h, w_ref[...], preferred_element_type=F32))
        if i > 0:
            epilogue(row_slices[i - 1], accs[i - 1])
    epilogue(row_slices[-1], accs[-1])


def _pre_attention(x, mod, mod_row, n1, w_qkv, qg, kg, hm, rope_tables, emit_cache, tm):
    b, t, d = x.shape
    rope = rope_tables is not None
    tok = lambda w: pl.BlockSpec((1, tm, w), lambda i, j: (i, j, 0))
    in_specs = [tok(d),
                pl.BlockSpec((1, N_MOD, d), lambda i, j: (mod_row(i), 0, 0)),
                _const_spec((1, d)), _const_spec(w_qkv.shape), _const_spec(qg.shape),
                _const_spec(kg.shape), _const_spec(hm.shape)]
    args = [x, mod, n1, w_qkv, qg, kg, hm]
    if rope:
        in_specs += [pl.BlockSpec((tm, LANES), lambda i, j: (j, 0))] * 2
        args += list(rope_tables)
    qt = (PAIRS, LANES)
    vt = (A_KV_HEADS, HEAD_DIM)
    layouts = [qt, 2 * LANES, vt, qt, B_WIDTH, B_WIDTH]
    out_specs = [tok(w) if isinstance(w, int) else
                 pl.BlockSpec((1,) + w + (tm,), lambda i, j: (i, 0, 0, j)) for w in layouts]
    out_shape = [jax.ShapeDtypeStruct((b, t, w) if isinstance(w, int) else (b,) + w + (t,), BF16)
                 for w in layouts]
    if emit_cache:
        cw = [A_KV_WIDTH, A_KV_WIDTH, B_WIDTH, B_WIDTH]
        out_specs += [tok(w) for w in cw]
        out_shape += [jax.ShapeDtypeStruct((b, t, w), F32) for w in cw]
    return pl.pallas_call(
        functools.partial(_pre_kernel, rope=rope, emit_cache=emit_cache),
        grid=(b, t // tm),
        in_specs=in_specs, out_specs=out_specs, out_shape=out_shape,
        compiler_params=pltpu.CompilerParams(dimension_semantics=("arbitrary", "arbitrary"),
                                             vmem_limit_bytes=VMEM_LIMIT),
        name="pre_attention",
    )(*args)


ONES_ROWS = 16


def _flash_t(streams, kv_sets, online, bias=None):
    items = [(si, start, chunk, st)
             for si, (n_keys, chunk, _, _) in enumerate(kv_sets)
             for start in range(0, n_keys, chunk)
             for st in range(len(streams))]

    def scores(item):
        si, start, size, st = item
        k = kv_sets[si][2](st, start, size)
        s = jnp.dot(k, streams[st][0], preferred_element_type=F32)
        if bias is not None and si == bias[0]:
            s = s + bias[1](st)
        return s

    carries = [None] * len(streams)
    pending = None
    for idx, item in enumerate(items):
        s = pending if pending is not None else scores(item)
        pending = None
        si, start, size, st = item
        if (online or carries[st] is None) and idx + 1 < len(items):
            pending = scores(items[idx + 1])
        ones = jnp.ones((ONES_ROWS, size), BF16)
        v_augs = [jnp.concatenate([vt, ones], axis=0) for vt in kv_sets[si][3](st, start, size)]
        alpha = None
        if carries[st] is None:
            shift, prev = jnp.max(s, axis=0, keepdims=True), None
        elif online:
            m_prev, prev = carries[st]
            shift = jnp.maximum(m_prev, jnp.max(s, axis=0, keepdims=True))
            alpha = jnp.exp2(m_prev - shift)
        else:
            shift, prev = carries[st]
        p = jnp.exp2(s - shift).astype(BF16)
        accs = []
        for gi, (c0, cn) in enumerate(streams[st][1]):
            acc = jnp.dot(v_augs[gi], p[:, c0:c0 + cn], preferred_element_type=F32)
            if prev is not None:
                acc = (prev[gi] if alpha is None else prev[gi] * alpha[:, c0:c0 + cn]) + acc
            accs.append(acc)
        carries[st] = (shift, accs)

    outs = [[acc[:HEAD_DIM] / acc[HEAD_DIM:HEAD_DIM + 1] for acc in accs] for _, accs in carries]
    flags = []
    for _, accs in carries:
        for acc in accs:
            bad = jnp.max(jnp.where(jnp.isfinite(acc), 0.0, 1.0), axis=0, keepdims=True)
            flags += [bad[:, c:c + LANES] for c in range(0, bad.shape[1], LANES)]
    return outs, functools.reduce(jnp.maximum, flags)


def _masked_heads(qt):
    lo_rows = lax.broadcasted_iota(jnp.int32, qt.shape, 0) < HEAD_DIM
    zero = jnp.zeros_like(qt)
    return jnp.where(lo_rows, qt, zero), jnp.where(lo_rows, zero, qt)


def _split_vt(v):
    vt = v.T
    return [vt[:HEAD_DIM], vt[HEAD_DIM:]]


EXCESS_TILE = (8, LANES)


def _store_tiles(o_ref, ex_ref, tiles, overflow):
    for p, o in enumerate(tiles):
        o_ref[0, :, p * LANES:(p + 1) * LANES] = o.astype(o_ref.dtype)
    if ex_ref is not None:
        ex_ref[0, 0, 0] = jnp.broadcast_to(overflow, EXCESS_TILE)


def _guarded(call):
    out, overflow = call(False)
    return lax.cond(jnp.max(overflow) == 0.0, lambda: out, lambda: call(True)[0])


def _attn_shared_kernel(q_ref, *refs, chunks, online, guard):
    o_ref, ex_ref = (refs[-2], refs[-1]) if guard else (refs[-1], None)
    kv = refs[:2 * len(chunks)]
    npairs, tq = q_ref.shape[1], q_ref.shape[3]
    kv_sets = [(kv[2 * i].shape[1], chunk,
                lambda st, start, size, r=kv[2 * i]: r[0, start:start + size, :],
                lambda st, start, size, r=kv[2 * i + 1]: [r[0, 0, :, start:start + size]])
               for i, chunk in enumerate(chunks)]
    masked = [_masked_heads(q_ref[0, p]) for p in range(npairs)]
    q_t = jnp.concatenate([m[0] for m in masked] + [m[1] for m in masked], axis=1)
    outs, overflow = _flash_t([(q_t, [(0, 2 * npairs * tq)])], kv_sets, online)
    o_t = outs[0][0]
    tiles = [jnp.concatenate([o_t[:, p * tq:(p + 1) * tq],
                              o_t[:, (npairs + p) * tq:(npairs + p + 1) * tq]], axis=0).T
             for p in range(npairs)]
    _store_tiles(o_ref, ex_ref, tiles, overflow)


def _attention_outputs(out_spec, out_sds, grid, guard):
    if not guard:
        return out_spec, out_sds
    ex_spec = pl.BlockSpec((1, 1, 1) + EXCESS_TILE, lambda i, c, j: (i, c, j, 0, 0))
    return [out_spec, ex_spec], [out_sds, jax.ShapeDtypeStruct(tuple(grid) + EXCESS_TILE, F32)]


def _shared_attention(qt, kv_sets, tq, chunks):
    b, _, _, t = qt.shape
    pairs_per_step = PAIRS // A_KV_HEADS
    in_specs = [pl.BlockSpec((1, pairs_per_step, LANES, tq), lambda i, c, j: (i, c, 0, j))]
    args = [qt]
    for k, vt in kv_sets:
        nk = k.shape[1]
        in_specs += [pl.BlockSpec((1, nk, LANES), lambda i, c, j: (i, 0, c)),
                     pl.BlockSpec((1, 1, HEAD_DIM, nk), lambda i, c, j: (i, c, 0, 0))]
        args += [k, vt]
    grid = (b, A_KV_HEADS, t // tq)
    guard = sum(k.shape[1] // chunk for (k, _), chunk in zip(kv_sets, chunks)) > 1
    out_specs, out_shape = _attention_outputs(
        pl.BlockSpec((1, tq, pairs_per_step * LANES), lambda i, c, j: (i, j, c)),
        jax.ShapeDtypeStruct((b, t, A_WIDTH), BF16), grid, guard)

    def call(online):
        return pl.pallas_call(
            functools.partial(_attn_shared_kernel, chunks=tuple(chunks), online=online, guard=guard),
            grid=grid, in_specs=in_specs, out_specs=out_specs, out_shape=out_shape,
            compiler_params=pltpu.CompilerParams(dimension_semantics=("arbitrary",) * 3,
                                                 vmem_limit_bytes=VMEM_LIMIT),
            name="shared_attention_online" if online else "shared_attention",
        )(*args)

    return _guarded(call) if guard else call(False)


def _pair_streams(q_ref):
    tq = q_ref.shape[3]
    return [(jnp.concatenate(_masked_heads(q_ref[0, p]), axis=1), [(0, tq), (tq, tq)])
            for p in range(q_ref.shape[1])]


def _pair_tiles(outs):
    return [jnp.concatenate(groups, axis=0).T for groups in outs]


def _pair_loaders(k_ref, v_ref, rows):
    lanes = lambda st: slice(st * LANES, (st + 1) * LANES)
    return (lambda st, start, size: k_ref[0, rows(start, size), lanes(st)],
            lambda st, start, size: _split_vt(v_ref[0, rows(start, size), lanes(st)]))


def _attn_full_kernel(q_ref, k_ref, v_ref, o_ref):
    nk = k_ref.shape[1]
    kv_sets = [(nk, nk) + _pair_loaders(k_ref, v_ref, lambda start, size: slice(start, start + size))]
    outs, overflow = _flash_t(_pair_streams(q_ref), kv_sets, False)
    _store_tiles(o_ref, None, _pair_tiles(outs), overflow)


def _full_attention(qt, k, v):
    b, npairs, _, t = qt.shape
    tok = pl.BlockSpec((1, t, npairs * LANES), lambda i: (i, 0, 0))
    return pl.pallas_call(
        _attn_full_kernel,
        grid=(b,),
        in_specs=[pl.BlockSpec((1, npairs, LANES, t), lambda i: (i, 0, 0, 0)), tok, tok],
        out_specs=tok,
        out_shape=jax.ShapeDtypeStruct((b, t, npairs * LANES), BF16),
        compiler_params=pltpu.CompilerParams(dimension_semantics=("arbitrary",),
                                             vmem_limit_bytes=VMEM_LIMIT),
        name="full_attention",
    )(qt, k, v)


NAT_PAIRS_PER_STEP = 4


def _attn_nat_kernel(q_ref, kw_ref, vw_ref, kc_ref, vc_ref, bias_ref, o_ref, ex_ref, *, rows, online):
    j = pl.program_id(2)
    ws = jnp.clip(j * NAT_Q_ROWS - WIN_H // 2, 0, rows - NAT_K_ROWS)
    win = pl.multiple_of(ws * GRID_W, GRID_W)
    nk = NAT_K_ROWS * GRID_W
    nc = kc_ref.shape[1]
    kv_sets = [(nc, nc) + _pair_loaders(kc_ref, vc_ref, lambda start, size: slice(start, start + size)),
               (nk, nk) + _pair_loaders(kw_ref, vw_ref, lambda start, size: pl.ds(win, nk))]
    load_bias = lambda st: jnp.concatenate([bias_ref[0, 2 * st], bias_ref[0, 2 * st + 1]], axis=1)

    outs, overflow = _flash_t(_pair_streams(q_ref), kv_sets, online, bias=(1, load_bias))
    _store_tiles(o_ref, ex_ref, _pair_tiles(outs), overflow)


def _nat_attention(qt, k, v, ctx_k, ctx_v, bias):
    b, _, _, t = qt.shape
    rows = t // GRID_W
    nblk = rows // NAT_Q_ROWS
    tq = NAT_Q_ROWS * GRID_W
    nk = NAT_K_ROWS * GRID_W
    width = NAT_PAIRS_PER_STEP * LANES
    whole = lambda n: pl.BlockSpec((1, n, width), lambda i, c, j: (i, 0, c))

    def bias_map(i, c, j):
        variant = jnp.where(j == 0, 0, jnp.where(j == nblk - 1, 2, 1))
        return (variant, c, 0, 0)

    grid = (b, PAIRS // NAT_PAIRS_PER_STEP, nblk)
    out_specs, out_shape = _attention_outputs(
        pl.BlockSpec((1, tq, width), lambda i, c, j: (i, j, c)),
        jax.ShapeDtypeStruct((b, t, B_WIDTH), BF16), grid, True)

    def call(online):
        return pl.pallas_call(
            functools.partial(_attn_nat_kernel, rows=rows, online=online),
            grid=grid,
            in_specs=[pl.BlockSpec((1, NAT_PAIRS_PER_STEP, LANES, tq), lambda i, c, j: (i, c, 0, j)),
                      whole(t), whole(t), whole(ctx_k.shape[1]), whole(ctx_v.shape[1]),
                      pl.BlockSpec((1, 2 * NAT_PAIRS_PER_STEP, nk, tq), bias_map)],
            out_specs=out_specs, out_shape=out_shape,
            compiler_params=pltpu.CompilerParams(dimension_semantics=("arbitrary",) * 3,
                                                 vmem_limit_bytes=VMEM_LIMIT),
            name="nat_attention_online" if online else "nat_attention",
        )(qt, k, v, ctx_k, ctx_v, bias)

    return _guarded(call)


def _post_kernel(x_ref, ao_ref, bo_ref, mod_ref, n1_ref, n2_ref, fg_ref,
                 wg_ref, wa_ref, wb_ref, wo_ref, w1_ref, w2_ref, y_ref):
    x = x_ref[0]
    mod = mod_ref[0]
    h = _rms_mod(x, n1_ref[...], mod[1:2], mod[0:1])
    gates = jnp.dot(h.astype(BF16), wg_ref[...], preferred_element_type=F32)
    ma = jnp.dot(ao_ref[0], wa_ref[...], preferred_element_type=F32)
    mb = jnp.dot(bo_ref[0], wb_ref[...], preferred_element_type=F32)
    m = jax.nn.sigmoid(gates[:, :D_MODEL]) * ma + jax.nn.sigmoid(gates[:, D_MODEL:]) * mb
    x1 = x + mod[2:3] * jnp.dot(m.astype(BF16), wo_ref[...], preferred_element_type=F32)
    h2 = _rms_mod(x1, n2_ref[...], mod[4:5], mod[3:4])
    u = jnp.maximum(jnp.dot(h2.astype(BF16), w1_ref[...], preferred_element_type=F32), 0.0)
    x2 = x1 + mod[5:6] * jnp.dot((u * u).astype(BF16), w2_ref[...], preferred_element_type=F32)
    inv = lax.rsqrt(jnp.mean(x2 * x2, axis=-1, keepdims=True) + RMS_EPS)
    y_ref[0] = (x2 * inv) * fg_ref[...]


def _post_attention(x, a_o, b_o, mod, mod_row, n1, n2, fg, wg, wa, wb, wo, w1, w2, tm):
    b, t, d = x.shape
    tok = lambda w: pl.BlockSpec((1, tm, w), lambda i, j: (i, j, 0))
    weights = [wg, wa, wb, wo, w1, w2]
    return pl.pallas_call(
        _post_kernel,
        grid=(b, t // tm),
        in_specs=[tok(d), tok(A_WIDTH), tok(B_WIDTH),
                  pl.BlockSpec((1, N_MOD, d), lambda i, j: (mod_row(i), 0, 0)),
                  _const_spec((1, d)), _const_spec((1, d)), _const_spec((1, d))]
                 + [_const_spec(w.shape) for w in weights],
        out_specs=tok(d),
        out_shape=jax.ShapeDtypeStruct(x.shape, F32),
        compiler_params=pltpu.CompilerParams(dimension_semantics=("arbitrary", "arbitrary"),
                                             vmem_limit_bytes=VMEM_LIMIT),
        name="post_attention",
    )(x, a_o, b_o, mod, n1, n2, fg, *weights)


def _rope_tables(t):
    pos = jnp.arange(t, dtype=jnp.int32)
    row = (pos // GRID_W).astype(F32)
    col = (pos % GRID_W).astype(F32)
    inv = ROPE_BASE ** (-jnp.arange(ROPE_PAIRS, dtype=F32) / ROPE_PAIRS)
    ar = row[:, None] * inv
    ac = col[:, None] * inv
    cos = jnp.concatenate([jnp.cos(ar), jnp.cos(ar), jnp.cos(ac), jnp.cos(ac)], axis=-1)
    sin = jnp.concatenate([-jnp.sin(ar), jnp.sin(ar), -jnp.sin(ac), jnp.sin(ac)], axis=-1)
    return jnp.tile(cos, (1, LANES // HEAD_DIM)), jnp.tile(sin, (1, LANES // HEAD_DIM))


def _dup_heads(x):
    b, t, kv, dh = x.shape
    return jnp.broadcast_to(x[:, :, :, None, :], (b, t, kv, 2, dh)).reshape(b, t, kv * 2 * dh).astype(BF16)


def kernel(x_prompt, x_sample, cache_a_k, cache_a_v, cache_b_k, cache_b_v, c, c_ctx, w_mod, b_mod,
           norm1_g, norm2_g, w_in, q_norm_g, k_norm_g, nat_bias, w_br_a, w_br_b, w_out, w_mlp_in,
           w_mlp_out, final_norm_g):
    assert w_mod.shape[0] == 1, "single-layer trunk"
    nb, seq, d = x_prompt.shape
    nd, dseq, _ = x_sample.shape
    past = cache_a_k.shape[2]

    ctx_row = nd
    pad = (-(nd + 1)) % 8
    cvec = jnp.concatenate([c, c_ctx[None, :], jnp.zeros((pad, d), F32)], axis=0)
    mod = _modulation(cvec, w_mod[0], b_mod[0]).reshape(cvec.shape[0], N_MOD, d)

    w_qkv = w_in[0, :, :QKV_WIDTH].astype(BF16)
    w_gate = w_in[0, :, QKV_WIDTH:].astype(BF16)
    wa = w_br_a[0].astype(BF16)
    wb = w_br_b[0].astype(BF16)
    wo = w_out[0].astype(BF16)
    w1 = w_mlp_in[0].astype(BF16)
    w2 = w_mlp_out[0].astype(BF16)
    n1 = norm1_g[0].reshape(1, d)
    n2 = norm2_g[0].reshape(1, d)
    fg = final_norm_g.reshape(1, d)
    qg = (jnp.tile(q_norm_g[0], A_HEADS) * Q_SCALE).reshape(1, A_WIDTH)
    kg = jnp.tile(k_norm_g[0], A_KV_HEADS).reshape(1, A_KV_WIDTH)
    head_id = np.arange(LANES) // HEAD_DIM
    hm = jnp.asarray(head_id[:, None] == head_id[None, :], dtype=BF16)

    bias = _expand_bias(nat_bias[0])

    ctx_mod_row = lambda i: ctx_row
    (qa, ka, va, qb, kb, vb, new_ak, new_av, new_bk, new_bv) = _pre_attention(
        x_prompt, mod, ctx_mod_row, n1, w_qkv, qg, kg, hm, None, True, seq)
    a_o = _shared_attention(qa, [(ka, va)], seq, [seq])
    b_o = _full_attention(qb, kb, vb)
    y_prompt = _post_attention(x_prompt, a_o, b_o, mod, ctx_mod_row, n1, n2, fg,
                               w_gate, wa, wb, wo, w1, w2, seq)

    lat_mod_row = lambda i: i
    (qa, ka, va, qb, kb, vb) = _pre_attention(
        x_sample, mod, lat_mod_row, n1, w_qkv, qg, kg, hm, _rope_tables(dseq), False, 1024)
    ca_k = _dup_heads(cache_a_k[:, 0])
    ca_vt = jnp.transpose(cache_a_v[:, 0], (0, 2, 3, 1)).astype(BF16)
    cb_k = cache_b_k[:, 0].reshape(nd, past, B_WIDTH).astype(BF16)
    cb_v = cache_b_v[:, 0].reshape(nd, past, B_WIDTH).astype(BF16)
    a_o = _shared_attention(qa, [(ca_k, ca_vt), (ka, va)], 1024, [256, 256])
    b_o = _nat_attention(qb, kb, vb, cb_k, cb_v, bias)
    y_sample = _post_attention(x_sample, a_o, b_o, mod, lat_mod_row, n1, n2, fg,
                               w_gate, wa, wb, wo, w1, w2, 512)

    return (y_prompt, y_sample,
            new_ak.reshape(nb, 1, seq, A_KV_HEADS, HEAD_DIM),
            new_av.reshape(nb, 1, seq, A_KV_HEADS, HEAD_DIM),
            new_bk.reshape(nb, 1, seq, B_HEADS, HEAD_DIM),
            new_bv.reshape(nb, 1, seq, B_HEADS, HEAD_DIM))
```

```python
import functools

import numpy as np
import jax
import jax.numpy as jnp
from jax import lax
from jax.experimental import pallas as pl
from jax.experimental.pallas import tpu as pltpu

F32 = jnp.float32
BF16 = jnp.bfloat16

D_MODEL = 1024
GRID_W = 64
HEAD_DIM = 64
A_HEADS = 8
A_KV_HEADS = 2
B_HEADS = 8
A_WIDTH = A_HEADS * HEAD_DIM
A_KV_WIDTH = A_KV_HEADS * HEAD_DIM
B_WIDTH = B_HEADS * HEAD_DIM
D_FF = 4 * D_MODEL
WIN_H = 8
WIN_W = 16
ROPE_BASE = 10000.0
ROPE_PAIRS = HEAD_DIM // 4
RMS_EPS = 1e-6
N_MOD = 6
NEG_INF = -1e30

LANES = 128
PAIRS = A_WIDTH // LANES
QKV_WIDTH = A_WIDTH + 2 * A_KV_WIDTH + 3 * B_WIDTH
GATE_WIDTH = 2 * D_MODEL
NAT_Q_ROWS = 4
NAT_K_ROWS = 12
PRE_SUB_ROWS = 256
PRE_TOKENS = 1024
GQA_QUERIES = 1024
GQA_KEY_CHUNK = 256
POST_TOKENS = 512
VMEM_LIMIT = 56 * 1024 * 1024
LOG2E = 1.4426950408889634
Q_SCALE = HEAD_DIM ** -0.5 * LOG2E


def _const_spec(shape):
    nd = len(shape)
    return pl.BlockSpec(shape, lambda *_: (0,) * nd, pipeline_mode=pl.Buffered(1))


def _rms_mod(x, g, scale, shift):
    inv = lax.rsqrt(jnp.mean(x * x, axis=-1, keepdims=True) + RMS_EPS)
    return (x * inv) * g * (1.0 + scale) + shift


def _mod_kernel(c_ref, w_ref, b_ref, o_ref):
    cv = c_ref[...]
    s = cv * jax.nn.sigmoid(cv)
    o_ref[...] = jnp.dot(s.astype(BF16), w_ref[...].astype(BF16),
                         preferred_element_type=F32) + b_ref[...]


def _modulation(cvec, w_mod, b_mod):
    n, d = cvec.shape
    width = w_mod.shape[1]
    tn = 1024
    return pl.pallas_call(
        _mod_kernel,
        grid=(width // tn,),
        in_specs=[pl.BlockSpec((n, d), lambda j: (0, 0)),
                  pl.BlockSpec((d, tn), lambda j: (0, j)),
                  pl.BlockSpec((1, tn), lambda j: (0, j))],
        out_specs=pl.BlockSpec((n, tn), lambda j: (0, j)),
        out_shape=jax.ShapeDtypeStruct((n, width), F32),
        compiler_params=pltpu.CompilerParams(dimension_semantics=("arbitrary",),
                                             vmem_limit_bytes=VMEM_LIMIT),
        name="modulation",
    )(cvec, w_mod, b_mod.reshape(1, width))


def _nat_variant_geometry(variant):
    if variant == 0:
        return 0, (lambda i, n: n < WIN_H)
    if variant == 1:
        return -(WIN_H // 2), (lambda i, n: 0 <= n - i < WIN_H)
    return -WIN_H, (lambda i, n: n >= NAT_K_ROWS - WIN_H)


def _bias_kernel(r_ref, o_ref):
    h = pl.program_id(0)
    n_dr = 2 * WIN_H - 1
    n_dc = 2 * WIN_W - 1
    kc = lax.broadcasted_iota(jnp.int32, (GRID_W, LANES), 0)
    lane = lax.broadcasted_iota(jnp.int32, (GRID_W, LANES), 1)
    qc = lane & (GRID_W - 1)
    hi_half = lane >= GRID_W
    dc = kc - qc + (WIN_W - 1)
    cstart = jnp.clip(qc - WIN_W // 2, 0, GRID_W - WIN_W)
    col_ok = (kc >= cstart) & (kc < cstart + WIN_W)
    neg = jnp.full((GRID_W, LANES), NEG_INF, F32)
    base = h * (n_dr * n_dc)
    tiles = []
    for a in range(n_dr):
        val = neg
        for b in range(n_dc):
            val = jnp.where(dc == b, r_ref[base + a * n_dc + b] * LOG2E, val)
        tiles.append(jnp.where(col_ok, val, neg))
    for variant in range(3):
        off, valid = _nat_variant_geometry(variant)
        for n in range(NAT_K_ROWS):
            for p in range(NAT_Q_ROWS // 2):
                halves = []
                for i in (2 * p, 2 * p + 1):
                    a = n - i + off + WIN_H - 1
                    halves.append(tiles[a] if valid(i, n) else neg)
                o_ref[variant, 0, n * GRID_W:(n + 1) * GRID_W, p * LANES:(p + 1) * LANES] = (
                    jnp.where(hi_half, halves[1], halves[0]))


def _expand_bias(nat_bias):
    heads = nat_bias.shape[0]
    nq = NAT_Q_ROWS * GRID_W
    nk = NAT_K_ROWS * GRID_W
    return pl.pallas_call(
        _bias_kernel,
        grid=(heads,),
        in_specs=[pl.BlockSpec(memory_space=pltpu.SMEM)],
        out_specs=pl.BlockSpec((3, 1, nk, nq), lambda h: (0, h, 0, 0)),
        out_shape=jax.ShapeDtypeStruct((3, heads, nk, nq), F32),
        compiler_params=pltpu.CompilerParams(dimension_semantics=("arbitrary",),
                                             vmem_limit_bytes=VMEM_LIMIT),
        name="bias_expand",
    )(nat_bias.reshape(-1))


def _pre_kernel(*refs, rope, emit_cache):
    x_ref, mod_ref, n1_ref, w_ref, qg_ref, kg_ref, hm_ref = refs[:7]
    pos = 7
    if rope:
        cos_ref, sin_ref = refs[pos:pos + 2]
        pos += 2
    qa_ref, ka_ref, va_ref, qb_ref, kb_ref, vb_ref = refs[pos:pos + 6]
    pos += 6
    if emit_cache:
        cak_ref, cav_ref, cbk_ref, cbv_ref = refs[pos:pos + 4]

    mod = mod_ref[0]
    tm = x_ref.shape[1]
    sub = min(tm, PRE_SUB_ROWS)
    row_slices = [slice(r, r + sub) for r in range(0, tm, sub)]
    hs = [_rms_mod(x_ref[0, rows], n1_ref[...], mod[1:2], mod[0:1]).astype(BF16) for rows in row_slices]
    lane = lax.broadcasted_iota(jnp.int32, (sub, LANES), 1)
    lo_half = lane < HEAD_DIM
    hm = hm_ref[...]

    def head_norm(y, g):
        ms = jnp.dot((y * y).astype(BF16), hm, preferred_element_type=F32) * (1.0 / HEAD_DIM)
        return (y * lax.rsqrt(ms + RMS_EPS)) * g

    def epilogue(rows, acc):
        def rotary(y):
            partner = jnp.where((lane & ROPE_PAIRS) == 0,
                                pltpu.roll(y, LANES - ROPE_PAIRS, 1), pltpu.roll(y, ROPE_PAIRS, 1))
            return y * cos_ref[rows] + partner * sin_ref[rows]

        def store_dup(ref, y):
            r = pltpu.roll(y, HEAD_DIM, 1)
            ref[0, rows, 0:LANES] = jnp.where(lo_half, y, r).astype(ref.dtype)
            ref[0, rows, LANES:2 * LANES] = jnp.where(lo_half, r, y).astype(ref.dtype)

        for c in range(PAIRS):
            sl = slice(c * LANES, (c + 1) * LANES)
            y = head_norm(acc[:, sl], qg_ref[:, sl])
            if rope:
                y = rotary(y)
            qa_ref[0, c, :, rows] = y.T.astype(qa_ref.dtype)

        k = head_norm(acc[:, A_WIDTH:A_WIDTH + A_KV_WIDTH], kg_ref[...])
        v = acc[:, A_WIDTH + A_KV_WIDTH:A_WIDTH + 2 * A_KV_WIDTH]
        if emit_cache:
            cak_ref[0, rows] = k
            cav_ref[0, rows] = v
        if rope:
            k = rotary(k)
        store_dup(ka_ref, k)
        vt = v.T
        for g in range(A_KV_HEADS):
            va_ref[0, g, :, rows] = vt[g * HEAD_DIM:(g + 1) * HEAD_DIM].astype(va_ref.dtype)

        b0 = A_WIDTH + 2 * A_KV_WIDTH
        bq = acc[:, b0:b0 + B_WIDTH]
        bk = acc[:, b0 + B_WIDTH:b0 + 2 * B_WIDTH]
        bv = acc[:, b0 + 2 * B_WIDTH:b0 + 3 * B_WIDTH]
        for c in range(PAIRS):
            qb_ref[0, c, :, rows] = (bq[:, c * LANES:(c + 1) * LANES] * Q_SCALE).T.astype(qb_ref.dtype)
        kb_ref[0, rows] = bk.astype(kb_ref.dtype)
        vb_ref[0, rows] = bv.astype(vb_ref.dtype)
        if emit_cache:
            cbk_ref[0, rows] = bk
            cbv_ref[0, rows] = bv

    accs = []
    for i, h in enumerate(hs):
        accs.append(jnp.dot(h, w_ref[...], preferred_element_type=F32))
        if i > 0:
            epilogue(row_slices[i - 1], accs[i - 1])
    epilogue(row_slices[-1], accs[-1])


def _pre_attention(x, mod, mod_row, n1, w_qkv, qg, kg, hm, rope_tables, emit_cache, tm):
    b, t, d = x.shape
    rope = rope_tables is not None
    tok = lambda w: pl.BlockSpec((1, tm, w), lambda i, j: (i, j, 0))
    in_specs = [tok(d),
                pl.BlockSpec((1, N_MOD, d), lambda i, j: (mod_row(i), 0, 0)),
                _const_spec((1, d)), _const_spec(w_qkv.shape), _const_spec(qg.shape),
                _const_spec(kg.shape), _const_spec(hm.shape)]
    args = [x, mod, n1, w_qkv, qg, kg, hm]
    if rope:
        in_specs += [pl.BlockSpec((tm, LANES), lambda i, j: (j, 0))] * 2
        args += list(rope_tables)
    qt = (PAIRS, LANES)
    vt = (A_KV_HEADS, HEAD_DIM)
    layouts = [qt, 2 * LANES, vt, qt, B_WIDTH, B_WIDTH]
    out_specs = [tok(w) if isinstance(w, int) else
                 pl.BlockSpec((1,) + w + (tm,), lambda i, j: (i, 0, 0, j)) for w in layouts]
    out_shape = [jax.ShapeDtypeStruct((b, t, w) if isinstance(w, int) else (b,) + w + (t,), BF16)
                 for w in layouts]
    if emit_cache:
        cw = [A_KV_WIDTH, A_KV_WIDTH, B_WIDTH, B_WIDTH]
        out_specs += [tok(w) for w in cw]
        out_shape += [jax.ShapeDtypeStruct((b, t, w), F32) for w in cw]
    return pl.pallas_call(
        functools.partial(_pre_kernel, rope=rope, emit_cache=emit_cache),
        grid=(b, t // tm),
        in_specs=in_specs, out_specs=out_specs, out_shape=out_shape,
        compiler_params=pltpu.CompilerParams(dimension_semantics=("arbitrary", "arbitrary"),
                                             vmem_limit_bytes=VMEM_LIMIT),
        name="pre_attention",
    )(*args)


ONES_ROWS = 16


def _flash_t(streams, kv_sets, online, bias=None):
    items = [(si, start, chunk, st)
             for si, (n_keys, chunk, _, _) in enumerate(kv_sets)
             for start in range(0, n_keys, chunk)
             for st in range(len(streams))]

    def scores(item):
        si, start, size, st = item
        k = kv_sets[si][2](st, start, size)
        s = jnp.dot(k, streams[st][0], preferred_element_type=F32)
        if bias is not None and si == bias[0]:
            s = s + bias[1](st)
        return s

    carries = [None] * len(streams)
    pending = None
    for idx, item in enumerate(items):
        s = pending if pending is not None else scores(item)
        pending = None
        si, start, size, st = item
        if (online or carries[st] is None) and idx + 1 < len(items):
            pending = scores(items[idx + 1])
        ones = jnp.ones((ONES_ROWS, size), BF16)
        v_augs = [jnp.concatenate([vt, ones], axis=0) for vt in kv_sets[si][3](st, start, size)]
        alpha = None
        if carries[st] is None:
            shift, prev = jnp.max(s, axis=0, keepdims=True), None
        elif online:
            m_prev, prev = carries[st]
            shift = jnp.maximum(m_prev, jnp.max(s, axis=0, keepdims=True))
            alpha = jnp.exp2(m_prev - shift)
        else:
            shift, prev = carries[st]
        p = jnp.exp2(s - shift).astype(BF16)
        accs = []
        for gi, (c0, cn) in enumerate(streams[st][1]):
            acc = jnp.dot(v_augs[gi], p[:, c0:c0 + cn], preferred_element_type=F32)
            if prev is not None:
                acc = (prev[gi] if alpha is None else prev[gi] * alpha[:, c0:c0 + cn]) + acc
            accs.append(acc)
        carries[st] = (shift, accs)

    outs = [[acc[:HEAD_DIM] / acc[HEAD_DIM:HEAD_DIM + 1] for acc in accs] for _, accs in carries]
    flags = []
    for _, accs in carries:
        for acc in accs:
            bad = jnp.max(jnp.where(jnp.isfinite(acc), 0.0, 1.0), axis=0, keepdims=True)
            flags += [bad[:, c:c + LANES] for c in range(0, bad.shape[1], LANES)]
    return outs, functools.reduce(jnp.maximum, flags)


def _masked_heads(qt):
    lo_rows = lax.broadcasted_iota(jnp.int32, qt.shape, 0) < HEAD_DIM
    zero = jnp.zeros_like(qt)
    return jnp.where(lo_rows, qt, zero), jnp.where(lo_rows, zero, qt)


def _split_vt(v):
    vt = v.T
    return [vt[:HEAD_DIM], vt[HEAD_DIM:]]


EXCESS_TILE = (8, LANES)


def _store_tiles(o_ref, ex_ref, tiles, overflow):
    for p, o in enumerate(tiles):
        o_ref[0, :, p * LANES:(p + 1) * LANES] = o.astype(o_ref.dtype)
    if ex_ref is not None:
        ex_ref[0, 0, 0] = jnp.broadcast_to(overflow, EXCESS_TILE)


def _guarded(call):
    out, overflow = call(False)
    return lax.cond(jnp.max(overflow) == 0.0, lambda: out, lambda: call(True)[0])


def _attn_shared_kernel(q_ref, *refs, chunks, online, guard):
    o_ref, ex_ref = (refs[-2], refs[-1]) if guard else (refs[-1], None)
    kv = refs[:2 * len(chunks)]
    npairs, tq = q_ref.shape[1], q_ref.shape[3]
    kv_sets = [(kv[2 * i].shape[1], chunk,
                lambda st, start, size, r=kv[2 * i]: r[0, start:start + size, :],
                lambda st, start, size, r=kv[2 * i + 1]: [r[0, 0, :, start:start + size]])
               for i, chunk in enumerate(chunks)]
    masked = [_masked_heads(q_ref[0, p]) for p in range(npairs)]
    q_t = jnp.concatenate([m[0] for m in masked] + [m[1] for m in masked], axis=1)
    outs, overflow = _flash_t([(q_t, [(0, 2 * npairs * tq)])], kv_sets, online)
    o_t = outs[0][0]
    tiles = [jnp.concatenate([o_t[:, p * tq:(p + 1) * tq],
                              o_t[:, (npairs + p) * tq:(npairs + p + 1) * tq]], axis=0).T
             for p in range(npairs)]
    _store_tiles(o_ref, ex_ref, tiles, overflow)


def _attention_outputs(out_spec, out_sds, grid, guard):
    if not guard:
        return out_spec, out_sds
    ex_spec = pl.BlockSpec((1, 1, 1) + EXCESS_TILE, lambda i, c, j: (i, c, j, 0, 0))
    return [out_spec, ex_spec], [out_sds, jax.ShapeDtypeStruct(tuple(grid) + EXCESS_TILE, F32)]


def _shared_attention(qt, kv_sets, tq, chunks):
    b, _, _, t = qt.shape
    pairs_per_step = PAIRS // A_KV_HEADS
    in_specs = [pl.BlockSpec((1, pairs_per_step, LANES, tq), lambda i, c, j: (i, c, 0, j))]
    args = [qt]
    for k, vt in kv_sets:
        nk = k.shape[1]
        in_specs += [pl.BlockSpec((1, nk, LANES), lambda i, c, j: (i, 0, c)),
                     pl.BlockSpec((1, 1, HEAD_DIM, nk), lambda i, c, j: (i, c, 0, 0))]
        args += [k, vt]
    grid = (b, A_KV_HEADS, t // tq)
    guard = sum(k.shape[1] // chunk for (k, _), chunk in zip(kv_sets, chunks)) > 1
    out_specs, out_shape = _attention_outputs(
        pl.BlockSpec((1, tq, pairs_per_step * LANES), lambda i, c, j: (i, j, c)),
        jax.ShapeDtypeStruct((b, t, A_WIDTH), BF16), grid, guard)

    def call(online):
        return pl.pallas_call(
            functools.partial(_attn_shared_kernel, chunks=tuple(chunks), online=online, guard=guard),
            grid=grid, in_specs=in_specs, out_specs=out_specs, out_shape=out_shape,
            compiler_params=pltpu.CompilerParams(dimension_semantics=("arbitrary",) * 3,
                                                 vmem_limit_bytes=VMEM_LIMIT),
            name="shared_attention_online" if online else "shared_attention",
        )(*args)

    return _guarded(call) if guard else call(False)


def _pair_streams(q_ref):
    tq = q_ref.shape[3]
    return [(jnp.concatenate(_masked_heads(q_ref[0, p]), axis=1), [(0, tq), (tq, tq)])
            for p in range(q_ref.shape[1])]


def _pair_tiles(outs):
    return [jnp.concatenate(groups, axis=0).T for groups in outs]


def _pair_loaders(k_ref, v_ref, rows):
    lanes = lambda st: slice(st * LANES, (st + 1) * LANES)
    return (lambda st, start, size: k_ref[0, rows(start, size), lanes(st)],
            lambda st, start, size: _split_vt(v_ref[0, rows(start, size), lanes(st)]))


def _attn_full_kernel(q_ref, k_ref, v_ref, o_ref):
    nk = k_ref.shape[1]
    kv_sets = [(nk, nk) + _pair_loaders(k_ref, v_ref, lambda start, size: slice(start, start + size))]
    outs, overflow = _flash_t(_pair_streams(q_ref), kv_sets, False)
    _store_tiles(o_ref, None, _pair_tiles(outs), overflow)


def _full_attention(qt, k, v):
    b, npairs, _, t = qt.shape
    tok = pl.BlockSpec((1, t, npairs * LANES), lambda i: (i, 0, 0))
    return pl.pallas_call(
        _attn_full_kernel,
        grid=(b,),
        in_specs=[pl.BlockSpec((1, npairs, LANES, t), lambda i: (i, 0, 0, 0)), tok, tok],
        out_specs=tok,
        out_shape=jax.ShapeDtypeStruct((b, t, npairs * LANES), BF16),
        compiler_params=pltpu.CompilerParams(dimension_semantics=("arbitrary",),
                                             vmem_limit_bytes=VMEM_LIMIT),
        name="full_attention",
    )(qt, k, v)


NAT_PAIRS_PER_STEP = 4


def _attn_nat_kernel(q_ref, kw_ref, vw_ref, kc_ref, vc_ref, bias_ref, o_ref, ex_ref, *, rows, online):
    j = pl.program_id(2)
    ws = jnp.clip(j * NAT_Q_ROWS - WIN_H // 2, 0, rows - NAT_K_ROWS)
    win = pl.multiple_of(ws * GRID_W, GRID_W)
    nk = NAT_K_ROWS * GRID_W
    nc = kc_ref.shape[1]
    kv_sets = [(nc, nc) + _pair_loaders(kc_ref, vc_ref, lambda start, size: slice(start, start + size)),
               (nk, nk) + _pair_loaders(kw_ref, vw_ref, lambda start, size: pl.ds(win, nk))]
    load_bias = lambda st: jnp.concatenate([bias_ref[0, 2 * st], bias_ref[0, 2 * st + 1]], axis=1)

    outs, overflow = _flash_t(_pair_streams(q_ref), kv_sets, online, bias=(1, load_bias))
    _store_tiles(o_ref, ex_ref, _pair_tiles(outs), overflow)


def _nat_attention(qt, k, v, ctx_k, ctx_v, bias):
    b, _, _, t = qt.shape
    rows = t // GRID_W
    nblk = rows // NAT_Q_ROWS
    tq = NAT_Q_ROWS * GRID_W
    nk = NAT_K_ROWS * GRID_W
    width = NAT_PAIRS_PER_STEP * LANES
    whole = lambda n: pl.BlockSpec((1, n, width), lambda i, c, j: (i, 0, c))

    def bias_map(i, c, j):
        variant = jnp.where(j == 0, 0, jnp.where(j == nblk - 1, 2, 1))
        return (variant, c, 0, 0)

    grid = (b, PAIRS // NAT_PAIRS_PER_STEP, nblk)
    out_specs, out_shape = _attention_outputs(
        pl.BlockSpec((1, tq, width), lambda i, c, j: (i, j, c)),
        jax.ShapeDtypeStruct((b, t, B_WIDTH), BF16), grid, True)

    def call(online):
        return pl.pallas_call(
            functools.partial(_attn_nat_kernel, rows=rows, online=online),
            grid=grid,
            in_specs=[pl.BlockSpec((1, NAT_PAIRS_PER_STEP, LANES, tq), lambda i, c, j: (i, c, 0, j)),
                      whole(t), whole(t), whole(ctx_k.shape[1]), whole(ctx_v.shape[1]),
                      pl.BlockSpec((1, 2 * NAT_PAIRS_PER_STEP, nk, tq), bias_map)],
            out_specs=out_specs, out_shape=out_shape,
            compiler_params=pltpu.CompilerParams(dimension_semantics=("arbitrary",) * 3,
                                                 vmem_limit_bytes=VMEM_LIMIT),
            name="nat_attention_online" if online else "nat_attention",
        )(qt, k, v, ctx_k, ctx_v, bias)

    return _guarded(call)


def _post_kernel(x_ref, ao_ref, bo_ref, mod_ref, n1_ref, n2_ref, fg_ref,
                 wg_ref, wa_ref, wb_ref, wo_ref, w1_ref, w2_ref, y_ref):
    x = x_ref[0]
    mod = mod_ref[0]
    h = _rms_mod(x, n1_ref[...], mod[1:2], mod[0:1])
    gates = jnp.dot(h.astype(BF16), wg_ref[...], preferred_element_type=F32)
    ma = jnp.dot(ao_ref[0], wa_ref[...], preferred_element_type=F32)
    mb = jnp.dot(bo_ref[0], wb_ref[...], preferred_element_type=F32)
    m = jax.nn.sigmoid(gates[:, :D_MODEL]) * ma + jax.nn.sigmoid(gates[:, D_MODEL:]) * mb
    x1 = x + mod[2:3] * jnp.dot(m.astype(BF16), wo_ref[...], preferred_element_type=F32)
    h2 = _rms_mod(x1, n2_ref[...], mod[4:5], mod[3:4])
    u = jnp.maximum(jnp.dot(h2.astype(BF16), w1_ref[...], preferred_element_type=F32), 0.0)
    x2 = x1 + mod[5:6] * jnp.dot((u * u).astype(BF16), w2_ref[...], preferred_element_type=F32)
    inv = lax.rsqrt(jnp.mean(x2 * x2, axis=-1, keepdims=True) + RMS_EPS)
    y_ref[0] = (x2 * inv) * fg_ref[...]


def _post_attention(x, a_o, b_o, mod, mod_row, n1, n2, fg, wg, wa, wb, wo, w1, w2, tm):
    b, t, d = x.shape
    tok = lambda w: pl.BlockSpec((1, tm, w), lambda i, j: (i, j, 0))
    weights = [wg, wa, wb, wo, w1, w2]
    return pl.pallas_call(
        _post_kernel,
        grid=(b, t // tm),
        in_specs=[tok(d), tok(A_WIDTH), tok(B_WIDTH),
                  pl.BlockSpec((1, N_MOD, d), lambda i, j: (mod_row(i), 0, 0)),
                  _const_spec((1, d)), _const_spec((1, d)), _const_spec((1, d))]
                 + [_const_spec(w.shape) for w in weights],
        out_specs=tok(d),
        out_shape=jax.ShapeDtypeStruct(x.shape, F32),
        compiler_params=pltpu.CompilerParams(dimension_semantics=("arbitrary", "arbitrary"),
                                             vmem_limit_bytes=VMEM_LIMIT),
        name="post_attention",
    )(x, a_o, b_o, mod, n1, n2, fg, *weights)


def _rope_tables(t):
    pos = jnp.arange(t, dtype=jnp.int32)
    row = (pos // GRID_W).astype(F32)
    col = (pos % GRID_W).astype(F32)
    inv = ROPE_BASE ** (-jnp.arange(ROPE_PAIRS, dtype=F32) / ROPE_PAIRS)
    ar = row[:, None] * inv
    ac = col[:, None] * inv
    cos = jnp.concatenate([jnp.cos(ar), jnp.cos(ar), jnp.cos(ac), jnp.cos(ac)], axis=-1)
    sin = jnp.concatenate([-jnp.sin(ar), jnp.sin(ar), -jnp.sin(ac), jnp.sin(ac)], axis=-1)
    return jnp.tile(cos, (1, LANES // HEAD_DIM)), jnp.tile(sin, (1, LANES // HEAD_DIM))


def _dup_heads(x):
    b, t, kv, dh = x.shape
    return jnp.broadcast_to(x[:, :, :, None, :], (b, t, kv, 2, dh)).reshape(b, t, kv * 2 * dh).astype(BF16)


def kernel(x_prompt, x_sample, cache_a_k, cache_a_v, cache_b_k, cache_b_v, c, c_ctx, w_mod, b_mod,
           norm1_g, norm2_g, w_in, q_norm_g, k_norm_g, nat_bias, w_br_a, w_br_b, w_out, w_mlp_in,
           w_mlp_out, final_norm_g):
    assert w_mod.shape[0] == 1, "single-layer trunk"
    nb, seq, d = x_prompt.shape
    nd, dseq, _ = x_sample.shape
    past = cache_a_k.shape[2]

    ctx_row = nd
    pad = (-(nd + 1)) % 8
    cvec = jnp.concatenate([c, c_ctx[None, :], jnp.zeros((pad, d), F32)], axis=0)
    mod = _modulation(cvec, w_mod[0], b_mod[0]).reshape(cvec.shape[0], N_MOD, d)

    w_qkv = w_in[0, :, :QKV_WIDTH].astype(BF16)
    w_gate = w_in[0, :, QKV_WIDTH:].astype(BF16)
    wa = w_br_a[0].astype(BF16)
    wb = w_br_b[0].astype(BF16)
    wo = w_out[0].astype(BF16)
    w1 = w_mlp_in[0].astype(BF16)
    w2 = w_mlp_out[0].astype(BF16)
    n1 = norm1_g[0].reshape(1, d)
    n2 = norm2_g[0].reshape(1, d)
    fg = final_norm_g.reshape(1, d)
    qg = (jnp.tile(q_norm_g[0], A_HEADS) * Q_SCALE).reshape(1, A_WIDTH)
    kg = jnp.tile(k_norm_g[0], A_KV_HEADS).reshape(1, A_KV_WIDTH)
    head_id = np.arange(LANES) // HEAD_DIM
    hm = jnp.asarray(head_id[:, None] == head_id[None, :], dtype=BF16)

    bias = _expand_bias(nat_bias[0])

    ctx_mod_row = lambda i: ctx_row
    (qa, ka, va, qb, kb, vb, new_ak, new_av, new_bk, new_bv) = _pre_attention(
        x_prompt, mod, ctx_mod_row, n1, w_qkv, qg, kg, hm, None, True, seq)
    a_o = _shared_attention(qa, [(ka, va)], seq, [seq])
    b_o = _full_attention(qb, kb, vb)
    y_prompt = _post_attention(x_prompt, a_o, b_o, mod, ctx_mod_row, n1, n2, fg,
                               w_gate, wa, wb, wo, w1, w2, seq)

    lat_mod_row = lambda i: i
    (qa, ka, va, qb, kb, vb) = _pre_attention(
        x_sample, mod, lat_mod_row, n1, w_qkv, qg, kg, hm, _rope_tables(dseq), False, min(PRE_TOKENS, dseq))
    ca_k = _dup_heads(cache_a_k[:, 0])
    ca_vt = jnp.transpose(cache_a_v[:, 0], (0, 2, 3, 1)).astype(BF16)
    cb_k = cache_b_k[:, 0].reshape(nd, past, B_WIDTH).astype(BF16)
    cb_v = cache_b_v[:, 0].reshape(nd, past, B_WIDTH).astype(BF16)
    a_o = _shared_attention(qa, [(ca_k, ca_vt), (ka, va)], min(GQA_QUERIES, dseq),
                            [GQA_KEY_CHUNK, GQA_KEY_CHUNK])
    b_o = _nat_attention(qb, kb, vb, cb_k, cb_v, bias)
    y_sample = _post_attention(x_sample, a_o, b_o, mod, lat_mod_row, n1, n2, fg,
                               w_gate, wa, wb, wo, w1, w2, min(POST_TOKENS, dseq))

    return (y_prompt, y_sample,
            new_ak.reshape(nb, 1, seq, A_KV_HEADS, HEAD_DIM),
            new_av.reshape(nb, 1, seq, A_KV_HEADS, HEAD_DIM),
            new_bk.reshape(nb, 1, seq, B_HEADS, HEAD_DIM),
            new_bv.reshape(nb, 1, seq, B_HEADS, HEAD_DIM))
```
